```python
import numpy as np
import jax
import jax.numpy as jnp
from jax import lax

D_MODEL = 1024
BATCH = 4
SEQ = 8192
DEPTH = 1

D_RNN = 1344
RNN_BLOCKS = 16
RNN_BLOCK_W = D_RNN // RNN_BLOCKS
RNN_CONV_W = 4
LRU_C = 8.0
N_HEADS = 16
N_KV = 4
HEAD_DIM = 64
GROUP = N_HEADS // N_KV
CMP_LEN = 32
CMP_STRIDE = 16
CMP_HIDDEN = 128
SEL_LEN = 64
N_SELECT = 16
WINDOW = 512
Q_BLOCK = 64
D_FF = 3 * D_MODEL
FFN_CONV_W = 3
EPS = 1e-6
NEG = -1e30
FORCE = 1e6

kernel_name = 'hybrid_rglru_nsa_convffn'


def rms_norm(x, g):
    x32 = x.astype(jnp.float32)
    y = x32 * lax.rsqrt(jnp.mean(x32 * x32, axis=-1, keepdims=True) + EPS)
    return (y * g.astype(jnp.float32)).astype(x.dtype)


def causal_dwconv(x, w, b):
    k = w.shape[0]
    y = lax.conv_general_dilated(x, w[:, None, :].astype(x.dtype), window_strides=(1,),
                                 padding=[(k - 1, 0)], dimension_numbers=('NWC', 'WIO', 'NWC'),
                                 feature_group_count=x.shape[-1])
    return y + b.astype(x.dtype)


def masked_softmax(s, mask):
    p = jax.nn.softmax(jnp.where(mask, s, NEG), axis=-1)
    return p * mask


def rg_lru(x, w_a, b_a, w_x, b_x, lam):
    bsz, s, _ = x.shape
    xb = x.reshape(bsz, s, RNN_BLOCKS, RNN_BLOCK_W)
    r = jax.nn.sigmoid(jnp.einsum('bsnc,ncd->bsnd', xb, w_a).reshape(bsz, s, D_RNN) + b_a)
    i = jax.nn.sigmoid(jnp.einsum('bsnc,ncd->bsnd', xb, w_x).reshape(bsz, s, D_RNN) + b_x)
    log_a = LRU_C * r.astype(jnp.float32) * jax.nn.log_sigmoid(lam.astype(jnp.float32))
    a = jnp.exp(log_a)
    u = jnp.sqrt(-jnp.expm1(2.0 * log_a)) * (i * x).astype(jnp.float32)

    def combine(lhs, rhs):
        a1, b1 = lhs
        a2, b2 = rhs
        return a1 * a2, a2 * b1 + b2

    _, h = lax.associative_scan(combine, (a, u), axis=1)
    return h.astype(x.dtype)


def compress(kv, pe, w1, w2):
    bsz, s, g, hd = kv.shape
    nc = (s - CMP_LEN) // CMP_STRIDE + 1
    pos = np.arange(nc)[:, None] * CMP_STRIDE + np.arange(CMP_LEN)[None, :]
    blocks = kv[:, pos] + pe[None, None, :, None, :]
    blocks = blocks.transpose(0, 3, 1, 2, 4).reshape(bsz, g, nc, CMP_LEN * hd)
    return jax.nn.gelu(blocks @ w1) @ w2


def overlap_matrix(nc, ns):
    i = np.arange(nc)[:, None]
    j = np.arange(ns)[None, :]
    lo = np.maximum(i * CMP_STRIDE, j * SEL_LEN)
    hi = np.minimum(i * CMP_STRIDE + CMP_LEN, (j + 1) * SEL_LEN)
    return jnp.asarray(np.maximum(hi - lo, 0) / CMP_STRIDE, dtype=jnp.float32)


def nsa(q, kc, vc, ks, vs, kw, vw, gates, q_norm_g, k_cmp_norm_g, k_slc_norm_g, k_win_norm_g,
        cmp_pe_k, cmp_w1_k, cmp_w2_k, cmp_pe_v, cmp_w1_v, cmp_w2_v):
    bsz, s, _ = q.shape
    f32 = jnp.float32
    scale = HEAD_DIM ** -0.5

    def heads(t, n):
        return t.reshape(bsz, s, n, HEAD_DIM)

    qh = rms_norm(heads(q, N_HEADS), q_norm_g)
    qh = qh.reshape(bsz, s, N_KV, GROUP, HEAD_DIM).transpose(0, 2, 3, 1, 4)
    k_cmp = rms_norm(compress(heads(kc, N_KV), cmp_pe_k, cmp_w1_k, cmp_w2_k), k_cmp_norm_g)
    v_cmp = compress(heads(vc, N_KV), cmp_pe_v, cmp_w1_v, cmp_w2_v)
    nc = k_cmp.shape[2]
    ns = s // SEL_LEN
    k_slc = rms_norm(heads(ks, N_KV), k_slc_norm_g).transpose(0, 2, 1, 3).reshape(bsz, N_KV, ns, SEL_LEN, HEAD_DIM)
    v_slc = heads(vs, N_KV).transpose(0, 2, 1, 3).reshape(bsz, N_KV, ns, SEL_LEN, HEAD_DIM)
    pad = ((0, 0), (0, 0), (WINDOW, 0), (0, 0))
    k_win = jnp.pad(rms_norm(heads(kw, N_KV), k_win_norm_g).transpose(0, 2, 1, 3), pad)
    v_win = jnp.pad(heads(vw, N_KV).transpose(0, 2, 1, 3), pad)
    g = jax.nn.sigmoid(gates).reshape(bsz, s, N_KV, GROUP, 3).transpose(0, 2, 3, 1, 4)
    overlap = overlap_matrix(nc, ns)
    cmp_end = np.arange(nc) * CMP_STRIDE + CMP_LEN - 1
    n_top = min(N_SELECT, ns)
    gather = jax.vmap(jax.vmap(lambda blk, ix: blk[ix]))
    jb = jnp.arange(ns)

    def block(qi):
        st = qi * Q_BLOCK
        t = st + jnp.arange(Q_BLOCK)
        qb = lax.dynamic_slice_in_dim(qh, st, Q_BLOCK, axis=3)
        gb = lax.dynamic_slice_in_dim(g, st, Q_BLOCK, axis=3)
        sc = jnp.einsum('bgrqd,bgcd->bgrqc', qb, k_cmp).astype(f32) * scale
        p_cmp = masked_softmax(sc, cmp_end[None, :] <= t[:, None])
        o_cmp = jnp.einsum('bgrqc,bgcd->bgrqd', p_cmp.astype(v_cmp.dtype), v_cmp)
        imp = jnp.einsum('bgrqc,cn->bgqn', p_cmp, overlap)
        cur = t // SEL_LEN
        causal_b = jb[None, :] <= cur[:, None]
        forced = (jb[None, :] == 0) | (jb[None, :] == cur[:, None]) | (jb[None, :] == cur[:, None] - 1)
        rank = jnp.where(causal_b, jnp.where(forced, FORCE, imp), -FORCE)
        _, idx = lax.top_k(rank, n_top)
        kg = gather(k_slc, idx)
        vg = gather(v_slc, idx)
        kpos = idx[..., None] * SEL_LEN + jnp.arange(SEL_LEN)
        ok = (idx <= cur[:, None])[..., None] & (kpos <= t[:, None, None])
        sc = jnp.einsum('bgrqd,bgqnld->bgrqnl', qb, kg).astype(f32) * scale
        p = masked_softmax(sc.reshape(bsz, N_KV, GROUP, Q_BLOCK, n_top * SEL_LEN),
                           ok.reshape(bsz, N_KV, 1, Q_BLOCK, n_top * SEL_LEN))
        o_slc = jnp.einsum('bgrqm,bgqmd->bgrqd', p.astype(vg.dtype),
                           vg.reshape(bsz, N_KV, Q_BLOCK, n_top * SEL_LEN, HEAD_DIM))
        kwb = lax.dynamic_slice_in_dim(k_win, st, WINDOW + Q_BLOCK, axis=2)
        vwb = lax.dynamic_slice_in_dim(v_win, st, WINDOW + Q_BLOCK, axis=2)
        kp = st - WINDOW + jnp.arange(WINDOW + Q_BLOCK)
        okw = (kp[None, :] <= t[:, None]) & (kp[None, :] > t[:, None] - WINDOW) & (kp[None, :] >= 0)
        sc = jnp.einsum('bgrqd,bgkd->bgrqk', qb, kwb).astype(f32) * scale
        pw = masked_softmax(sc, okw)
        o_win = jnp.einsum('bgrqk,bgkd->bgrqd', pw.astype(vwb.dtype), vwb)
        return gb[..., 0:1] * o_cmp + gb[..., 1:2] * o_slc + gb[..., 2:3] * o_win

    out = lax.map(block, jnp.arange(s // Q_BLOCK))
    return out.transpose(1, 0, 4, 2, 3, 5).reshape(bsz, s, N_HEADS * HEAD_DIM)


def setup_inputs(seed: int = 0) -> dict:
    key = jax.random.key(seed)
    k = jax.random.split(key, 28)

    def nrm(kk, shape, scale):
        return jax.random.normal(kk, shape, jnp.float32) * scale

    d, hd = D_MODEL, HEAD_DIM
    n_in = 2 * D_RNN + N_HEADS * hd + 6 * N_KV * hd + 3 * N_HEADS + 2 * d
    u = jax.random.uniform(k[9], (D_RNN,), jnp.float32, 0.9, 0.999)
    sg = u ** (1.0 / LRU_C)
    lam = jnp.log(sg) - jnp.log1p(-sg)
    return {
        'x': nrm(k[0], (BATCH, SEQ, d), 1.0),
        'norm1_g': 1.0 + nrm(k[1], (d,), 0.02),
        'w_in': nrm(k[2], (d, n_in), d ** -0.5),
        'rnn_conv_w': nrm(k[3], (RNN_CONV_W, D_RNN), RNN_CONV_W ** -0.5),
        'rnn_conv_b': nrm(k[4], (D_RNN,), 0.02),
        'rg_w_a': nrm(k[5], (RNN_BLOCKS, RNN_BLOCK_W, RNN_BLOCK_W), RNN_BLOCK_W ** -0.5),
        'rg_b_a': nrm(k[6], (D_RNN,), 0.02),
        'rg_w_x': nrm(k[7], (RNN_BLOCKS, RNN_BLOCK_W, RNN_BLOCK_W), RNN_BLOCK_W ** -0.5),
        'rg_b_x': nrm(k[8], (D_RNN,), 0.02),
        'lru_lambda': lam,
        'w_rnn_out': nrm(k[10], (D_RNN, d), D_RNN ** -0.5),
        'q_norm_g': 1.0 + nrm(k[11], (hd,), 0.02),
        'k_cmp_norm_g': 1.0 + nrm(k[12], (hd,), 0.02),
        'k_slc_norm_g': 1.0 + nrm(k[13], (hd,), 0.02),
        'k_win_norm_g': 1.0 + nrm(k[14], (hd,), 0.02),
        'cmp_pe_k': nrm(k[15], (CMP_LEN, hd), 0.1),
        'cmp_w1_k': nrm(k[16], (CMP_LEN * hd, CMP_HIDDEN), (CMP_LEN * hd) ** -0.5),
        'cmp_w2_k': nrm(k[17], (CMP_HIDDEN, hd), CMP_HIDDEN ** -0.5),
        'cmp_pe_v': nrm(k[18], (CMP_LEN, hd), 0.1),
        'cmp_w1_v': nrm(k[19], (CMP_LEN * hd, CMP_HIDDEN), (CMP_LEN * hd) ** -0.5),
        'cmp_w2_v': nrm(k[20], (CMP_HIDDEN, hd), CMP_HIDDEN ** -0.5),
        'w_attn_out': nrm(k[21], (N_HEADS * hd, d), (N_HEADS * hd) ** -0.5),
        'w_out': nrm(k[22], (d, d), d ** -0.5),
        'norm2_g': 1.0 + nrm(k[23], (d,), 0.02),
        'w_up': nrm(k[24], (d, 2 * D_FF), d ** -0.5),
        'ffn_conv_w': nrm(k[25], (FFN_CONV_W, 2 * D_FF), FFN_CONV_W ** -0.5),
        'ffn_conv_b': nrm(k[26], (2 * D_FF,), 0.02),
        'w_down': nrm(k[27], (D_FF, d), D_FF ** -0.5),
    }


def reference(x, norm1_g, w_in, rnn_conv_w, rnn_conv_b, rg_w_a, rg_b_a, rg_w_x, rg_b_x, lru_lambda,
              w_rnn_out, q_norm_g, k_cmp_norm_g, k_slc_norm_g, k_win_norm_g, cmp_pe_k, cmp_w1_k, cmp_w2_k,
              cmp_pe_v, cmp_w1_v, cmp_w2_v, w_attn_out, w_out, norm2_g, w_up, ffn_conv_w, ffn_conv_b, w_down):
    kvw = N_KV * HEAD_DIM
    sizes = [D_RNN, D_RNN, N_HEADS * HEAD_DIM] + [kvw] * 6 + [3 * N_HEADS, D_MODEL, D_MODEL]
    splits = np.cumsum(sizes)[:-1].tolist()
    for _ in range(DEPTH):
        h = rms_norm(x, norm1_g)
        proj = h @ w_in
        (rx, rgate, q, kc, vc, ks, vs, kw, vw, nsa_g, mg_rnn, mg_attn) = jnp.split(proj, splits, axis=-1)
        hr = rg_lru(causal_dwconv(rx, rnn_conv_w, rnn_conv_b), rg_w_a, rg_b_a, rg_w_x, rg_b_x, lru_lambda)
        y_rnn = (hr * jax.nn.gelu(rgate)) @ w_rnn_out
        o = nsa(q, kc, vc, ks, vs, kw, vw, nsa_g, q_norm_g, k_cmp_norm_g, k_slc_norm_g, k_win_norm_g,
                cmp_pe_k, cmp_w1_k, cmp_w2_k, cmp_pe_v, cmp_w1_v, cmp_w2_v)
        y_attn = o @ w_attn_out
        x = x + (jax.nn.sigmoid(mg_rnn) * y_rnn + jax.nn.sigmoid(mg_attn) * y_attn) @ w_out
        up = causal_dwconv(rms_norm(x, norm2_g) @ w_up, ffn_conv_w, ffn_conv_b)
        gate, val = jnp.split(up, 2, axis=-1)
        x = x + (jax.nn.gelu(gate) * val) @ w_down
    return x
```

```python
import functools

import numpy as np
import jax
import jax.numpy as jnp
from jax import lax
from jax.experimental import pallas as pl
from jax.experimental.pallas import tpu as pltpu

F32 = jnp.float32
BF16 = jnp.bfloat16

D_MODEL = 1024
D_RNN = 1344
RNN_BLOCKS = 16
RNN_CONV_W = 4
LRU_C = 8.0
N_HEADS = 16
N_KV = 4
HEAD_DIM = 64
GROUP = N_HEADS // N_KV
KV_W = N_KV * HEAD_DIM
CMP_LEN = 32
CMP_STRIDE = 16
CMP_HIDDEN = 128
SEL_LEN = 64
N_SELECT = 16
WINDOW = 512
D_FF = 3 * D_MODEL
FFN_CONV_W = 3
EPS = 1e-6
NEG = -1e30
FORCE = 1e6
SCALE = HEAD_DIM ** -0.5

SEL_LANES = 128
TQ = 128
SEL_CHUNK = 512
HALO = 8
VMEM_LIMIT = 56 * 1024 * 1024


def _cparams(*sem):
    return pltpu.CompilerParams(dimension_semantics=sem, vmem_limit_bytes=VMEM_LIMIT)


def _gelu(x):
    c = np.float32(np.sqrt(2.0 / np.pi))
    return x * (0.5 * (1.0 + jnp.tanh(c * (x + 0.044715 * (x * x * x)))))


def _sigmoid(x):
    return 1.0 / (1.0 + jnp.exp(-x))


def _rms(x, g):
    ms = jnp.mean(x * x, axis=-1, keepdims=True)
    return x * lax.rsqrt(ms + EPS) * g


def _dot(a, b):
    return jnp.dot(a, b, preferred_element_type=F32)


def _split_bf16(x):
    hi = x.astype(BF16)
    lo = (x - hi.astype(F32)).astype(BF16)
    return hi, lo


def _tile(n, want):
    t = min(n, want)
    while n % t:
        t //= 2
    return t


def _full(shape):
    nd = len(shape)
    return pl.BlockSpec(shape, lambda *_: (0,) * nd)


def _in_proj_kernel(x_ref, g_ref, w_rx, w_rg, w_q, w_kv, w_g, w_mg,
                    o_rx, o_rg, o_q, o_kv, o_g, o_mg):
    h = _rms(x_ref[...], g_ref[...]).astype(BF16)
    for w, o in ((w_rx, o_rx), (w_rg, o_rg), (w_q, o_q), (w_kv, o_kv), (w_g, o_g), (w_mg, o_mg)):
        o[...] = _dot(h, w[...])


def _in_proj(x2, norm1_g, w_in):
    t = x2.shape[0]
    tm = _tile(t, 256)
    kvw = 6 * KV_W
    o0 = 2 * D_RNN
    o1 = o0 + N_HEADS * HEAD_DIM
    o2 = o1 + kvw
    o3 = o2 + 3 * N_HEADS
    wb = w_in.astype(BF16)
    w_g = jnp.pad(wb[:, o2:o3], ((0, 0), (0, 128 - 3 * N_HEADS)))
    ws = [wb[:, :D_RNN], wb[:, D_RNN:o0], wb[:, o0:o1], wb[:, o1:o2], w_g, wb[:, o3:]]
    widths = [w.shape[1] for w in ws]
    return pl.pallas_call(
        _in_proj_kernel,
        grid=(t // tm,),
        in_specs=[pl.BlockSpec((tm, D_MODEL), lambda i: (i, 0)), _full((1, D_MODEL))]
        + [_full((D_MODEL, n)) for n in widths],
        out_specs=[pl.BlockSpec((tm, n), lambda i: (i, 0)) for n in widths],
        out_shape=[jax.ShapeDtypeStruct((t, n), F32) for n in widths],
        compiler_params=_cparams("parallel"),
        name="in_proj",
    )(x2, norm1_g.reshape(1, D_MODEL), *ws)


def _rglru_kernel(rx_ref, rg_ref, cw_ref, cb_ref, wa_ref, ba_ref, wx_ref, bx_ref, lam_ref, wo_ref,
                  y_ref, xs_ref, a_ref, b_ref, carry_ref, *, tt):
    ti = pl.program_id(1)

    @pl.when(ti == 0)
    def _():
        xs_ref[0:HALO, :] = jnp.zeros((HALO, D_RNN), F32)
        carry_ref[...] = jnp.zeros((HALO, D_RNN), F32)

    xs_ref[HALO:HALO + tt, :] = rx_ref[0]
    cw = cw_ref[...]
    x = cb_ref[...] + cw[RNN_CONV_W - 1:RNN_CONV_W, :] * xs_ref[HALO:HALO + tt, :]
    for k in range(RNN_CONV_W - 1):
        off = HALO - (RNN_CONV_W - 1) + k
        x = x + cw[k:k + 1, :] * xs_ref[off:off + tt, :]
    xs_ref[0:HALO, :] = xs_ref[tt:tt + HALO, :]

    xb = x.astype(BF16)
    r = _sigmoid(_dot(xb, wa_ref[...]) + ba_ref[...])
    gi = _sigmoid(_dot(xb, wx_ref[...]) + bx_ref[...])
    lam = lam_ref[...]
    log_sig = jnp.minimum(lam, 0.0) - jnp.log1p(jnp.exp(-jnp.abs(lam)))
    log_a = LRU_C * r * log_sig
    a = jnp.exp(log_a)
    u = jnp.sqrt(1.0 - a * a) * (gi * x)

    sub = lax.broadcasted_iota(jnp.int32, (tt, 1), 0) & (HALO - 1)
    for d in (1, 2, 4):
        keep = sub >= d
        a_sh = pltpu.roll(a, d, 0)
        u_sh = pltpu.roll(u, d, 0)
        u = jnp.where(keep, a * u_sh + u, u)
        a = jnp.where(keep, a * a_sh, a)
    a_ref[...] = a
    b_ref[...] = u

    def group(gidx, carry):
        rows = pl.ds(pl.multiple_of(gidx * HALO, HALO), HALO)
        h = a_ref[rows, :] * carry + b_ref[rows, :]
        b_ref[rows, :] = h
        return jnp.broadcast_to(h[HALO - 1:HALO, :], (HALO, D_RNN))

    carry_ref[...] = lax.fori_loop(0, tt // HALO, group, carry_ref[...], unroll=4)

    hg = (b_ref[...] * _gelu(rg_ref[0])).astype(BF16)
    y_ref[0] = _dot(hg, wo_ref[...])


def _block_diag(w):
    n, c, d = w.shape
    return jnp.einsum("ncd,nm->ncmd", w, jnp.eye(n, dtype=w.dtype)).reshape(n * c, n * d)


def _rglru(rx3, rg3, conv_w, conv_b, w_a, b_a, w_x, b_x, lam, w_out):
    b, s, _ = rx3.shape
    tt = _tile(s, 256)
    row = lambda v: v.reshape(1, D_RNN)
    return pl.pallas_call(
        functools.partial(_rglru_kernel, tt=tt),
        grid=(b, s // tt),
        in_specs=[pl.BlockSpec((1, tt, D_RNN), lambda i, j: (i, j, 0)),
                  pl.BlockSpec((1, tt, D_RNN), lambda i, j: (i, j, 0)),
                  _full((RNN_CONV_W, D_RNN)), _full((1, D_RNN)),
                  _full((D_RNN, D_RNN)), _full((1, D_RNN)),
                  _full((D_RNN, D_RNN)), _full((1, D_RNN)),
                  _full((1, D_RNN)), _full((D_RNN, D_MODEL))],
        out_specs=pl.BlockSpec((1, tt, D_MODEL), lambda i, j: (i, j, 0)),
        out_shape=jax.ShapeDtypeStruct((b, s, D_MODEL), F32),
        scratch_shapes=[pltpu.VMEM((tt + HALO, D_RNN), F32), pltpu.VMEM((tt, D_RNN), F32),
                        pltpu.VMEM((tt, D_RNN), F32), pltpu.VMEM((HALO, D_RNN), F32)],
        compiler_params=_cparams("arbitrary", "arbitrary"),
        name="rglru",
    )(rx3, rg3, conv_w, row(conv_b), _block_diag(w_a).astype(BF16), row(b_a),
      _block_diag(w_x).astype(BF16), row(b_x), row(lam), w_out.astype(BF16))


def _kv_prep_kernel(kv_ref, gs_ref, gw_ref, ks_ref, vs_ref, kw_ref, vw_ref):
    kv = kv_ref[...]
    for h in range(N_KV):
        lo = h * HEAD_DIM
        ks_ref[:, lo:lo + HEAD_DIM] = _rms(kv[:, 2 * KV_W + lo:2 * KV_W + lo + HEAD_DIM], gs_ref[...]).astype(BF16)
        kw_ref[:, lo:lo + HEAD_DIM] = _rms(kv[:, 4 * KV_W + lo:4 * KV_W + lo + HEAD_DIM], gw_ref[...]).astype(BF16)
    vs_ref[...] = kv[:, 3 * KV_W:4 * KV_W].astype(BF16)
    vw_ref[...] = kv[:, 5 * KV_W:6 * KV_W].astype(BF16)


def _kv_prep(kv, g_slc, g_win):
    t = kv.shape[0]
    tm = _tile(t, 512)
    out = pl.BlockSpec((tm, KV_W), lambda i: (i, 0))
    return pl.pallas_call(
        _kv_prep_kernel,
        grid=(t // tm,),
        in_specs=[pl.BlockSpec((tm, 6 * KV_W), lambda i: (i, 0)), _full((1, HEAD_DIM)), _full((1, HEAD_DIM))],
        out_specs=[out] * 4,
        out_shape=[jax.ShapeDtypeStruct((t, KV_W), BF16)] * 4,
        compiler_params=_cparams("parallel"),
        name="kv_prep",
    )(kv, g_slc.reshape(1, HEAD_DIM), g_win.reshape(1, HEAD_DIM))


def _compress_kernel(rk_ref, rv_ref, pek_ref, pev_ref, w1k_ref, w1v_ref, w2k_ref, w2v_ref, gk_ref,
                     kc_ref, vc_ref):
    half = CMP_STRIDE * HEAD_DIM
    ncp = rk_ref.shape[2]

    def mlp(r_ref, pe_ref, w1_ref, w2_ref):
        r = r_ref[0, 0]
        pe = pe_ref[...]
        lo = _dot((r + pe[:, :half]).astype(BF16), w1_ref[:half, :])
        hi = _dot((r + pe[:, half:]).astype(BF16), w1_ref[half:, :])
        hid = lo + pltpu.roll(hi, ncp - 1, 0)
        return _dot(_gelu(hid).astype(BF16), w2_ref[...])

    kc_ref[0, 0] = _rms(mlp(rk_ref, pek_ref, w1k_ref, w2k_ref), gk_ref[...])
    vc_ref[0, 0] = mlp(rv_ref, pev_ref, w1v_ref, w2v_ref)


def _compress(rk, rv, pe_k, pe_v, w1k, w1v, w2k, w2v, g_cmp):
    b, g, ncp, wide = rk.shape
    blk = pl.BlockSpec((1, 1, ncp, wide), lambda i, j: (i, j, 0, 0))
    out = pl.BlockSpec((1, 1, ncp, HEAD_DIM), lambda i, j: (i, j, 0, 0))
    w1 = _full((CMP_LEN * HEAD_DIM, CMP_HIDDEN))
    w2 = _full((CMP_HIDDEN, HEAD_DIM))
    pe = _full((1, CMP_LEN * HEAD_DIM))
    return pl.pallas_call(
        _compress_kernel,
        grid=(b, g),
        in_specs=[blk, blk, pe, pe, w1, w1, w2, w2, _full((1, HEAD_DIM))],
        out_specs=[out, out],
        out_shape=[jax.ShapeDtypeStruct((b, g, ncp, HEAD_DIM), F32)] * 2,
        compiler_params=_cparams("parallel", "parallel"),
        name="compress",
    )(rk, rv, pe_k.reshape(1, -1), pe_v.reshape(1, -1), w1k.astype(BF16), w1v.astype(BF16),
      w2k.astype(BF16), w2v.astype(BF16), g_cmp.reshape(1, HEAD_DIM))


def _softmax_rows(s):
    m = jnp.max(s, axis=-1, keepdims=True)
    e = jnp.exp(s - m)
    return e, jnp.sum(e, axis=-1, keepdims=True)


def _attn_kernel(q_ref, gt_ref, e_ref, qg_ref, kct_ref, vc_ref, ov_ref, kat_ref, vs_ref, kwt_ref, vw_ref,
                 o_ref):
    qi = pl.program_id(2)
    st = qi * TQ
    rows = GROUP * TQ
    ncp = kct_ref.shape[3]

    qt = q_ref[0]
    qg = qg_ref[...]
    qs = jnp.concatenate(
        [_rms(qt[:, r * HEAD_DIM:(r + 1) * HEAD_DIM], qg) * SCALE for r in range(GROUP)], axis=0)
    qsb = qs.astype(BF16)
    t_row = st + (lax.broadcasted_iota(jnp.int32, (rows, 1), 0) & (TQ - 1))

    s = _dot(qsb, kct_ref[0, 0])
    cidx = lax.broadcasted_iota(jnp.int32, (1, ncp), 1)
    valid = (cidx * CMP_STRIDE + (CMP_LEN - 1)) <= t_row
    e, _ = _softmax_rows(jnp.where(valid, s, NEG))
    e = jnp.where(valid, e, 0.0)
    l = jnp.sum(e, axis=-1, keepdims=True)
    p = e / jnp.where(l > 0.0, l, 1.0)
    o_cmp = _dot(p.astype(BF16), vc_ref[0, 0])

    psum = p[0:TQ] + p[TQ:2 * TQ] + p[2 * TQ:3 * TQ] + p[3 * TQ:4 * TQ]
    p_hi, p_lo = _split_bf16(psum)
    imp = _dot(p_hi, ov_ref[...]) + _dot(p_lo, ov_ref[...])
    jb = lax.broadcasted_iota(jnp.int32, (1, SEL_LANES), 1)
    jbf = jb.astype(F32)
    cur = (st + lax.broadcasted_iota(jnp.int32, (TQ, 1), 0)) // SEL_LEN
    forced = (jb == 0) | (jb == cur) | (jb == cur - 1)
    work = jnp.where(jb <= cur, jnp.where(forced, FORCE, imp), -FORCE)
    sel = jnp.zeros((TQ, SEL_LANES), jnp.bool_)
    for _ in range(N_SELECT):
        mx = jnp.max(work, axis=-1, keepdims=True)
        first = jnp.min(jnp.where(work == mx, jbf, float(SEL_LANES)), axis=-1, keepdims=True)
        pick = jbf == first
        sel = sel | pick
        work = jnp.where(pick, -jnp.inf, work)

    kpos = st + lax.broadcasted_iota(jnp.int32, (1, TQ), 1)
    sd = _dot(qsb, kat_ref[0, 0, SEL_LANES:SEL_LANES + HEAD_DIM, pl.ds(pl.multiple_of(st, TQ), TQ)])
    sd = jnp.where(kpos <= t_row, sd, NEG)
    m0 = jnp.max(sd, axis=-1, keepdims=True)
    p0 = jnp.exp(sd - m0)
    l0 = jnp.sum(p0, axis=-1, keepdims=True)
    acc0 = _dot(p0.astype(BF16), vs_ref[0, 0, pl.ds(pl.multiple_of(st, TQ), TQ), :])

    bias = jnp.where(sel & (jb < st // SEL_LEN), 0.0, NEG).astype(BF16)
    qaug = jnp.concatenate([jnp.concatenate([bias] * GROUP, axis=0), qsb], axis=1)

    def chunk(c, carry):
        m, l, acc = carry
        keys = pl.ds(pl.multiple_of(c * SEL_CHUNK, SEL_CHUNK), SEL_CHUNK)
        sc = _dot(qaug, kat_ref[0, 0, :, keys])
        mn = jnp.maximum(m, jnp.max(sc, axis=-1, keepdims=True))
        alpha = jnp.exp(m - mn)
        pc = jnp.exp(sc - mn)
        l = alpha * l + jnp.sum(pc, axis=-1, keepdims=True)
        acc = alpha * acc + _dot(pc.astype(BF16), vs_ref[0, 0, keys, :])
        return mn, l, acc

    _, l_s, acc_s = lax.fori_loop(0, (st + SEL_CHUNK - 1) // SEL_CHUNK, chunk, (m0, l0, acc0))
    o_slc = acc_s / l_s

    wkeys = pl.ds(pl.multiple_of(st, TQ), WINDOW + TQ)
    kp = st - WINDOW + lax.broadcasted_iota(jnp.int32, (1, WINDOW + TQ), 1)
    okw = (kp <= t_row) & (kp > t_row - WINDOW) & (kp >= 0)
    ew, lw = _softmax_rows(jnp.where(okw, _dot(qsb, kwt_ref[0, 0, :, wkeys]), NEG))
    o_win = _dot(ew.astype(BF16), vw_ref[0, 0, wkeys, :]) / lw

    g_hi, g_lo = _split_bf16(_sigmoid(gt_ref[0]))
    branches = (o_cmp, o_slc, o_win)
    for r in range(GROUP):
        acc = jnp.zeros((TQ, HEAD_DIM), F32)
        for j in range(3):
            ex = e_ref[0, j * GROUP + r]
            gate = _dot(g_hi, ex) + _dot(g_lo, ex)
            acc = acc + gate * branches[j][r * TQ:(r + 1) * TQ]
        o_ref[0, :, r * HEAD_DIM:(r + 1) * HEAD_DIM] = acc


def _attention(q3, gates3, expand, q_norm_g, kcmp_t, vcmp, overlap, kaug_t, vslc, kwin_t, vwin):
    b, s, _ = q3.shape
    ncp = kcmp_t.shape[3]
    sp = kwin_t.shape[3]
    per_bg = lambda shape: pl.BlockSpec((1, 1) + shape, lambda i, j, k: (i, j, 0, 0))
    return pl.pallas_call(
        _attn_kernel,
        grid=(b, N_KV, s // TQ),
        in_specs=[pl.BlockSpec((1, TQ, KV_W), lambda i, j, k: (i, k, j)),
                  pl.BlockSpec((1, TQ, SEL_LANES), lambda i, j, k: (i, k, 0)),
                  pl.BlockSpec((1, 3 * GROUP, SEL_LANES, HEAD_DIM), lambda i, j, k: (j, 0, 0, 0)),
                  _full((1, HEAD_DIM)),
                  per_bg((HEAD_DIM, ncp)), per_bg((ncp, HEAD_DIM)), _full((ncp, SEL_LANES)),
                  per_bg((SEL_LANES + HEAD_DIM, s)), per_bg((s, HEAD_DIM)),
                  per_bg((HEAD_DIM, sp)), per_bg((sp, HEAD_DIM))],
        out_specs=pl.BlockSpec((1, TQ, KV_W), lambda i, j, k: (i, k, j)),
        out_shape=jax.ShapeDtypeStruct((b, s, N_HEADS * HEAD_DIM), F32),
        compiler_params=_cparams("parallel", "parallel", "arbitrary"),
        name="nsa_attention",
    )(q3, gates3, expand, q_norm_g.reshape(1, HEAD_DIM), kcmp_t, vcmp, overlap, kaug_t, vslc, kwin_t, vwin)


def _overlap_matrix(s):
    ncp = s // CMP_STRIDE
    i = np.arange(ncp)[:, None]
    j = np.arange(SEL_LANES)[None, :]
    lo = np.maximum(i * CMP_STRIDE, j * SEL_LEN)
    hi = np.minimum(i * CMP_STRIDE + CMP_LEN, (j + 1) * SEL_LEN)
    ov = np.maximum(hi - lo, 0) / CMP_STRIDE
    ov = ov * (i < (s - CMP_LEN) // CMP_STRIDE + 1) * (j < s // SEL_LEN)
    return jnp.asarray(ov, dtype=BF16)


def _gate_expand():
    e = np.zeros((N_KV, 3 * GROUP, SEL_LANES, HEAD_DIM), np.float32)
    for g in range(N_KV):
        for r in range(GROUP):
            for j in range(3):
                e[g, j * GROUP + r, (g * GROUP + r) * 3 + j, :] = 1.0
    return jnp.asarray(e, dtype=BF16)


def _block_onehot_t(s):
    oh = (np.arange(SEL_LANES)[:, None] == (np.arange(s)[None, :] // SEL_LEN)).astype(np.float32)
    return jnp.asarray(oh, dtype=BF16)


def _merge_kernel(x_ref, yr_ref, o_ref, mg_ref, wa_ref, wo_ref, g2_ref, x1_ref, hn_ref):
    y_attn = _dot(o_ref[...].astype(BF16), wa_ref[...])
    mg = mg_ref[...]
    m = _sigmoid(mg[:, :D_MODEL]) * yr_ref[...] + _sigmoid(mg[:, D_MODEL:]) * y_attn
    x1 = x_ref[...] + _dot(m.astype(BF16), wo_ref[...])
    x1_ref[...] = x1
    hn_ref[...] = _rms(x1, g2_ref[...]).astype(BF16)


def _merge(x2, y_rnn, o_attn, mg, w_attn_out, w_out, norm2_g):
    t = x2.shape[0]
    tm = _tile(t, 512)
    rowblk = lambda n: pl.BlockSpec((tm, n), lambda i: (i, 0))
    return pl.pallas_call(
        _merge_kernel,
        grid=(t // tm,),
        in_specs=[rowblk(D_MODEL), rowblk(D_MODEL), rowblk(D_MODEL), rowblk(2 * D_MODEL),
                  _full((D_MODEL, D_MODEL)), _full((D_MODEL, D_MODEL)), _full((1, D_MODEL))],
        out_specs=[rowblk(D_MODEL), rowblk(D_MODEL)],
        out_shape=[jax.ShapeDtypeStruct((t, D_MODEL), F32), jax.ShapeDtypeStruct((t, D_MODEL), BF16)],
        compiler_params=_cparams("parallel"),
        name="merge",
    )(x2, y_rnn, o_attn, mg, w_attn_out.astype(BF16), w_out.astype(BF16), norm2_g.reshape(1, D_MODEL))


def _ffn_up_kernel(hp_ref, h_ref, wg_ref, wv_ref, cwg_ref, cwv_ref, cbg_ref, cbv_ref, act_ref, up_ref,
                   *, tm, tiles_per_seq):
    first = (pl.program_id(0) % tiles_per_seq) == 0
    keep = jnp.where(first, 0.0, 1.0)

    def conv(w_ref, cw_ref, cb_ref):
        up_ref[0:HALO, :] = _dot(hp_ref[...], w_ref[...]) * keep
        up_ref[HALO:HALO + tm, :] = _dot(h_ref[...], w_ref[...])
        cw = cw_ref[...]
        y = cb_ref[...] + cw[FFN_CONV_W - 1:FFN_CONV_W, :] * up_ref[HALO:HALO + tm, :]
        for k in range(FFN_CONV_W - 1):
            off = HALO - (FFN_CONV_W - 1) + k
            y = y + cw[k:k + 1, :] * up_ref[off:off + tm, :]
        return y

    gate = conv(wg_ref, cwg_ref, cbg_ref)
    val = conv(wv_ref, cwv_ref, cbv_ref)
    act_ref[...] = (_gelu(gate) * val).astype(BF16)


def _ffn_up(hn, w_up, conv_w, conv_b, s):
    t = hn.shape[0]
    tm = _tile(s, 1024)
    tn = 512
    nj = D_FF // tn
    wb = w_up.astype(BF16)
    cb = conv_b.reshape(1, 2 * D_FF)
    return pl.pallas_call(
        functools.partial(_ffn_up_kernel, tm=tm, tiles_per_seq=s // tm),
        grid=(t // tm, nj),
        in_specs=[pl.BlockSpec((HALO, D_MODEL), lambda i, j: (jnp.maximum(i * (tm // HALO) - 1, 0), 0)),
                  pl.BlockSpec((tm, D_MODEL), lambda i, j: (i, 0)),
                  pl.BlockSpec((D_MODEL, tn), lambda i, j: (0, j)),
                  pl.BlockSpec((D_MODEL, tn), lambda i, j: (0, j + nj)),
                  pl.BlockSpec((FFN_CONV_W, tn), lambda i, j: (0, j)),
                  pl.BlockSpec((FFN_CONV_W, tn), lambda i, j: (0, j + nj)),
                  pl.BlockSpec((1, tn), lambda i, j: (0, j)),
                  pl.BlockSpec((1, tn), lambda i, j: (0, j + nj))],
        out_specs=pl.BlockSpec((tm, tn), lambda i, j: (i, j)),
        out_shape=jax.ShapeDtypeStruct((t, D_FF), BF16),
        scratch_shapes=[pltpu.VMEM((tm + HALO, tn), F32)],
        compiler_params=_cparams("parallel", "arbitrary"),
        name="ffn_up",
    )(hn, hn, wb, wb, conv_w, conv_w, cb, cb)


def _ffn_down_kernel(x1_ref, act_ref, w_ref, o_ref):
    o_ref[...] = x1_ref[...] + _dot(act_ref[...], w_ref[...])


def _ffn_down(x1, act, w_down):
    t = x1.shape[0]
    tm = _tile(t, 512)
    return pl.pallas_call(
        _ffn_down_kernel,
        grid=(t // tm,),
        in_specs=[pl.BlockSpec((tm, D_MODEL), lambda i: (i, 0)), pl.BlockSpec((tm, D_FF), lambda i: (i, 0)),
                  _full((D_FF, D_MODEL))],
        out_specs=pl.BlockSpec((tm, D_MODEL), lambda i: (i, 0)),
        out_shape=jax.ShapeDtypeStruct((t, D_MODEL), F32),
        compiler_params=_cparams("parallel"),
        name="ffn_down",
    )(x1, act, w_down.astype(BF16))


def _heads_t(a, b, s):
    return a.reshape(b, s, N_KV, HEAD_DIM).transpose(0, 2, 3, 1)


def _heads(a, b, s):
    return a.reshape(b, s, N_KV, HEAD_DIM).transpose(0, 2, 1, 3)


def _cmp_rows(a, b, s):
    a = a.reshape(b, s // CMP_STRIDE, CMP_STRIDE, N_KV, HEAD_DIM).transpose(0, 3, 1, 2, 4)
    return a.reshape(b, N_KV, s // CMP_STRIDE, CMP_STRIDE * HEAD_DIM)


def kernel(x, norm1_g, w_in, rnn_conv_w, rnn_conv_b, rg_w_a, rg_b_a, rg_w_x, rg_b_x, lru_lambda, w_rnn_out, q_norm_g, k_cmp_norm_g, k_slc_norm_g, k_win_norm_g, cmp_pe_k, cmp_w1_k, cmp_w2_k, cmp_pe_v, cmp_w1_v, cmp_w2_v, w_attn_out, w_out, norm2_g, w_up, ffn_conv_w, ffn_conv_b, w_down):
    b, s, d = x.shape
    assert d == D_MODEL and s % SEL_CHUNK == 0 and s // SEL_LEN <= SEL_LANES
    t = b * s
    x2 = x.reshape(t, d)

    rx, rgate, q, kv, gates, mg = _in_proj(x2, norm1_g, w_in)

    y_rnn = _rglru(rx.reshape(b, s, D_RNN), rgate.reshape(b, s, D_RNN), rnn_conv_w, rnn_conv_b,
                   rg_w_a, rg_b_a, rg_w_x, rg_b_x, lru_lambda, w_rnn_out).reshape(t, d)

    ks, vs, kw, vw = _kv_prep(kv, k_slc_norm_g, k_win_norm_g)
    kcmp, vcmp = _compress(_cmp_rows(kv[:, :KV_W], b, s), _cmp_rows(kv[:, KV_W:2 * KV_W], b, s),
                           cmp_pe_k, cmp_pe_v, cmp_w1_k, cmp_w1_v, cmp_w2_k, cmp_w2_v, k_cmp_norm_g)

    onehot = jnp.broadcast_to(_block_onehot_t(s), (b, N_KV, SEL_LANES, s))
    kaug_t = jnp.concatenate([onehot, _heads_t(ks, b, s)], axis=2)
    kwin_t = jnp.pad(_heads_t(kw, b, s), ((0, 0), (0, 0), (0, 0), (WINDOW, 0)))
    vwin = jnp.pad(_heads(vw, b, s), ((0, 0), (0, 0), (WINDOW, 0), (0, 0)))
    o_attn = _attention(q.reshape(b, s, d), gates.reshape(b, s, SEL_LANES), _gate_expand(), q_norm_g,
                        kcmp.transpose(0, 1, 3, 2).astype(BF16), vcmp.astype(BF16), _overlap_matrix(s),
                        kaug_t, _heads(vs, b, s), kwin_t, vwin).reshape(t, d)

    x1, hn = _merge(x2, y_rnn, o_attn, mg, w_attn_out, w_out, norm2_g)
    act = _ffn_up(hn, w_up, ffn_conv_w, ffn_conv_b, s)
    return _ffn_down(x1, act, w_down).reshape(b, s, d)
```

```python
import functools

import numpy as np
import jax
import jax.numpy as jnp
from jax import lax
from jax.experimental import pallas as pl
from jax.experimental.pallas import tpu as pltpu

F32 = jnp.float32
BF16 = jnp.bfloat16

D_MODEL = 1024
D_RNN = 1344
RNN_BLOCKS = 16
RNN_CONV_W = 4
LRU_C = 8.0
N_HEADS = 16
N_KV = 4
HEAD_DIM = 64
GROUP = N_HEADS // N_KV
KV_W = N_KV * HEAD_DIM
CMP_LEN = 32
CMP_STRIDE = 16
CMP_HIDDEN = 128
SEL_LEN = 64
N_SELECT = 16
N_FORCED = 3
WINDOW = 512
D_FF = 3 * D_MODEL
FFN_CONV_W = 3
EPS = 1e-6
NEG = -1e30
FORCE = 1e6
SCALE = HEAD_DIM ** -0.5

SEL_LANES = 128
V_LANES = 128
QK_LANES = 128
GATE_LANES = 128
TQ = 256
SEL_TRIP = 1024
LOG2E = float(np.log2(np.e))
HALO = 8
VMEM_LIMIT = 56 * 1024 * 1024
STATIC_SCORE_BOUND = 40.0


def _cparams(*sem):
    return pltpu.CompilerParams(dimension_semantics=sem, vmem_limit_bytes=VMEM_LIMIT)


def _gelu(x):
    c = np.float32(np.sqrt(2.0 / np.pi))
    return x * (0.5 * (1.0 + jnp.tanh(c * (x + 0.044715 * (x * x * x)))))


def _sigmoid(x):
    return 1.0 / (1.0 + jnp.exp(-x))


def _rms(x, g):
    ms = jnp.mean(x * x, axis=-1, keepdims=True)
    return x * lax.rsqrt(ms + EPS) * g


def _dot(a, b):
    return jnp.dot(a, b, preferred_element_type=F32)


def _dot_nt(a, b):
    return lax.dot_general(a, b, (((1,), (1,)), ((), ())), preferred_element_type=F32)


def _split_bf16(x):
    hi = x.astype(BF16)
    lo = (x - hi.astype(F32)).astype(BF16)
    return hi, lo


def _tile(n, want):
    t = min(n, want)
    while n % t:
        t //= 2
    return t


def _full(shape):
    nd = len(shape)
    return pl.BlockSpec(shape, lambda *_: (0,) * nd)


def _in_proj_kernel(x_ref, g_ref, w_rx, w_rg, w_q, w_kv, w_g, w_mg,
                    o_rx, o_rg, o_q, o_kv, o_g, o_mg):
    h = _rms(x_ref[...], g_ref[...]).astype(BF16)
    for w, o in ((w_rx, o_rx), (w_rg, o_rg), (w_q, o_q), (w_kv, o_kv), (w_g, o_g), (w_mg, o_mg)):
        o[...] = _dot(h, w[...])


def _in_proj(x2, norm1_g, w_in):
    t = x2.shape[0]
    tm = _tile(t, 256)
    kvw = 6 * KV_W
    o0 = 2 * D_RNN
    o1 = o0 + N_HEADS * HEAD_DIM
    o2 = o1 + kvw
    o3 = o2 + 3 * N_HEADS
    wb = w_in.astype(BF16)
    w_g = jnp.pad(wb[:, o2:o3].reshape(D_MODEL, N_KV, 3 * GROUP), ((0, 0), (0, 0), (0, GATE_LANES - 3 * GROUP)))
    w_g = w_g.reshape(D_MODEL, N_KV * GATE_LANES)
    ws = [wb[:, :D_RNN], wb[:, D_RNN:o0], wb[:, o0:o1], wb[:, o1:o2], w_g, wb[:, o3:]]
    widths = [w.shape[1] for w in ws]
    return pl.pallas_call(
        _in_proj_kernel,
        grid=(t // tm,),
        in_specs=[pl.BlockSpec((tm, D_MODEL), lambda i: (i, 0)), _full((1, D_MODEL))]
        + [_full((D_MODEL, n)) for n in widths],
        out_specs=[pl.BlockSpec((tm, n), lambda i: (i, 0)) for n in widths],
        out_shape=[jax.ShapeDtypeStruct((t, n), F32) for n in widths],
        compiler_params=_cparams("parallel"),
        name="in_proj",
    )(x2, norm1_g.reshape(1, D_MODEL), *ws)


def _rglru_kernel(rx_ref, rg_ref, cw_ref, cb_ref, wa_ref, ba_ref, wx_ref, bx_ref, lam_ref, wo_ref,
                  y_ref, xs_ref, a_ref, b_ref, carry_ref, *, tt):
    ti = pl.program_id(1)

    @pl.when(ti == 0)
    def _():
        xs_ref[0:HALO, :] = jnp.zeros((HALO, D_RNN), F32)
        carry_ref[...] = jnp.zeros((HALO, D_RNN), F32)

    xs_ref[HALO:HALO + tt, :] = rx_ref[0]
    cw = cw_ref[...]
    x = cb_ref[...] + cw[RNN_CONV_W - 1:RNN_CONV_W, :] * xs_ref[HALO:HALO + tt, :]
    for k in range(RNN_CONV_W - 1):
        off = HALO - (RNN_CONV_W - 1) + k
        x = x + cw[k:k + 1, :] * xs_ref[off:off + tt, :]
    xs_ref[0:HALO, :] = xs_ref[tt:tt + HALO, :]

    xb = x.astype(BF16)
    r = _sigmoid(_dot(xb, wa_ref[...]) + ba_ref[...])
    gi = _sigmoid(_dot(xb, wx_ref[...]) + bx_ref[...])
    lam = lam_ref[...]
    log_sig = jnp.minimum(lam, 0.0) - jnp.log1p(jnp.exp(-jnp.abs(lam)))
    log_a = LRU_C * r * log_sig
    a = jnp.exp(log_a)
    u = jnp.sqrt(1.0 - a * a) * (gi * x)

    sub = lax.broadcasted_iota(jnp.int32, (tt, 1), 0) & (HALO - 1)
    for d in (1, 2, 4):
        keep = sub >= d
        a_sh = pltpu.roll(a, d, 0)
        u_sh = pltpu.roll(u, d, 0)
        u = jnp.where(keep, a * u_sh + u, u)
        a = jnp.where(keep, a * a_sh, a)
    a_ref[...] = a
    b_ref[...] = u

    def group(gidx, carry):
        rows = pl.ds(pl.multiple_of(gidx * HALO, HALO), HALO)
        h = a_ref[rows, :] * carry + b_ref[rows, :]
        b_ref[rows, :] = h
        return jnp.broadcast_to(h[HALO - 1:HALO, :], (HALO, D_RNN))

    carry_ref[...] = lax.fori_loop(0, tt // HALO, group, carry_ref[...], unroll=4)

    hg = (b_ref[...] * _gelu(rg_ref[0])).astype(BF16)
    y_ref[0] = _dot(hg, wo_ref[...])


def _block_diag(w):
    n, c, d = w.shape
    return jnp.einsum("ncd,nm->ncmd", w, jnp.eye(n, dtype=w.dtype)).reshape(n * c, n * d)


def _rglru(rx3, rg3, conv_w, conv_b, w_a, b_a, w_x, b_x, lam, w_out):
    b, s, _ = rx3.shape
    tt = _tile(s, 256)
    row = lambda v: v.reshape(1, D_RNN)
    return pl.pallas_call(
        functools.partial(_rglru_kernel, tt=tt),
        grid=(b, s // tt),
        in_specs=[pl.BlockSpec((1, tt, D_RNN), lambda i, j: (i, j, 0)),
                  pl.BlockSpec((1, tt, D_RNN), lambda i, j: (i, j, 0)),
                  _full((RNN_CONV_W, D_RNN)), _full((1, D_RNN)),
                  _full((D_RNN, D_RNN)), _full((1, D_RNN)),
                  _full((D_RNN, D_RNN)), _full((1, D_RNN)),
                  _full((1, D_RNN)), _full((D_RNN, D_MODEL))],
        out_specs=pl.BlockSpec((1, tt, D_MODEL), lambda i, j: (i, j, 0)),
        out_shape=jax.ShapeDtypeStruct((b, s, D_MODEL), F32),
        scratch_shapes=[pltpu.VMEM((tt + HALO, D_RNN), F32), pltpu.VMEM((tt, D_RNN), F32),
                        pltpu.VMEM((tt, D_RNN), F32), pltpu.VMEM((HALO, D_RNN), F32)],
        compiler_params=_cparams("arbitrary", "arbitrary"),
        name="rglru",
    )(rx3, rg3, conv_w, row(conv_b), _block_diag(w_a).astype(BF16), row(b_a),
      _block_diag(w_x).astype(BF16), row(b_x), row(lam), w_out.astype(BF16))


def _kv_prep_kernel(kv_ref, gs_ref, gw_ref, ks_ref, vs_ref, kw_ref, vw_ref):
    kv = kv_ref[...]
    tm = kv.shape[0]
    ones = jnp.ones((tm, V_LANES - HEAD_DIM), BF16)
    for h in range(N_KV):
        lo = h * HEAD_DIM
        ks_ref[:, lo:lo + HEAD_DIM] = _rms(kv[:, 2 * KV_W + lo:2 * KV_W + lo + HEAD_DIM], gs_ref[...]).astype(BF16)
        kw_ref[:, lo:lo + HEAD_DIM] = _rms(kv[:, 4 * KV_W + lo:4 * KV_W + lo + HEAD_DIM], gw_ref[...]).astype(BF16)
        for src, dst in ((3, vs_ref), (5, vw_ref)):
            v = kv[:, src * KV_W + lo:src * KV_W + lo + HEAD_DIM].astype(BF16)
            base = h * V_LANES
            dst[:, base:base + HEAD_DIM] = v
            dst[:, base + HEAD_DIM:base + V_LANES] = ones


def _kv_prep(kv, g_slc, g_win):
    t = kv.shape[0]
    tm = _tile(t, 512)
    kblk = pl.BlockSpec((tm, KV_W), lambda i: (i, 0))
    vblk = pl.BlockSpec((tm, N_KV * V_LANES), lambda i: (i, 0))
    kshape = jax.ShapeDtypeStruct((t, KV_W), BF16)
    vshape = jax.ShapeDtypeStruct((t, N_KV * V_LANES), BF16)
    return pl.pallas_call(
        _kv_prep_kernel,
        grid=(t // tm,),
        in_specs=[pl.BlockSpec((tm, 6 * KV_W), lambda i: (i, 0)), _full((1, HEAD_DIM)), _full((1, HEAD_DIM))],
        out_specs=[kblk, vblk, kblk, vblk],
        out_shape=[kshape, vshape, kshape, vshape],
        compiler_params=_cparams("parallel"),
        name="kv_prep",
    )(kv, g_slc.reshape(1, HEAD_DIM), g_win.reshape(1, HEAD_DIM))


def _compress_kernel(rk_ref, rv_ref, pek_ref, pev_ref, w1k_ref, w1v_ref, w2k_ref, w2v_ref, gk_ref,
                     kc_ref, vc_ref):
    half = CMP_STRIDE * HEAD_DIM
    ncp = rk_ref.shape[2]

    def mlp(r_ref, pe_ref, w1_ref, w2_ref):
        r = r_ref[0, 0]
        pe = pe_ref[...]
        lo = _dot((r + pe[:, :half]).astype(BF16), w1_ref[:half, :])
        hi = _dot((r + pe[:, half:]).astype(BF16), w1_ref[half:, :])
        hid = lo + pltpu.roll(hi, ncp - 1, 0)
        return _dot(_gelu(hid).astype(BF16), w2_ref[...])

    kc_ref[0, 0] = _rms(mlp(rk_ref, pek_ref, w1k_ref, w2k_ref), gk_ref[...])
    vc_ref[0, 0] = mlp(rv_ref, pev_ref, w1v_ref, w2v_ref)


def _compress(rk, rv, pe_k, pe_v, w1k, w1v, w2k, w2v, g_cmp):
    b, g, ncp, wide = rk.shape
    blk = pl.BlockSpec((1, 1, ncp, wide), lambda i, j: (i, j, 0, 0))
    out = pl.BlockSpec((1, 1, ncp, HEAD_DIM), lambda i, j: (i, j, 0, 0))
    w1 = _full((CMP_LEN * HEAD_DIM, CMP_HIDDEN))
    w2 = _full((CMP_HIDDEN, HEAD_DIM))
    pe = _full((1, CMP_LEN * HEAD_DIM))
    return pl.pallas_call(
        _compress_kernel,
        grid=(b, g),
        in_specs=[blk, blk, pe, pe, w1, w1, w2, w2, _full((1, HEAD_DIM))],
        out_specs=[out, out],
        out_shape=[jax.ShapeDtypeStruct((b, g, ncp, HEAD_DIM), F32)] * 2,
        compiler_params=_cparams("parallel", "parallel"),
        name="compress",
    )(rk, rv, pe_k.reshape(1, -1), pe_v.reshape(1, -1), w1k.astype(BF16), w1v.astype(BF16),
      w2k.astype(BF16), w2v.astype(BF16), g_cmp.reshape(1, HEAD_DIM))


def _exp_terms(sm, static):
    if static:
        return jnp.exp2(sm)
    return jnp.exp2(sm - jnp.max(sm, axis=-1, keepdims=True))


def _per_head(s, bias):
    return (s.reshape(GROUP, TQ, s.shape[-1]) + bias[None]).reshape(s.shape)


def _normalise(acc):
    swapped = pltpu.roll(acc, HEAD_DIM, 1)
    parts = []
    for r in range(GROUP):
        a, b = acc[r * TQ:(r + 1) * TQ], swapped[r * TQ:(r + 1) * TQ]
        parts.append(a * (1.0 / b) if r % 2 == 0 else b * (1.0 / a))
    return jnp.concatenate(parts, axis=0)


def _assemble(y):
    low = lax.broadcasted_iota(jnp.int32, (1, 2 * HEAD_DIM), 1) < HEAD_DIM
    pair = lambda r: jnp.where(low, y[r * TQ:(r + 1) * TQ], y[(r + 1) * TQ:(r + 2) * TQ])
    return jnp.concatenate([pair(0), pair(2)], axis=1)


def _q_prep_kernel(q_ref, qg_ref, o_ref):
    qt = q_ref[0]
    tail = (lax.broadcasted_iota(jnp.int32, (TQ, QK_LANES - HEAD_DIM), 1) == 0).astype(BF16)
    for g in range(N_KV):
        for r in range(GROUP):
            lo = (g * GROUP + r) * HEAD_DIM
            y = _rms(qt[:, lo:lo + HEAD_DIM], qg_ref[...]) * (SCALE * LOG2E)
            o_ref[0, g, 0, r * TQ:(r + 1) * TQ, 0:HEAD_DIM] = y.astype(BF16)
            o_ref[0, g, 0, r * TQ:(r + 1) * TQ, HEAD_DIM:QK_LANES] = tail


def _q_prep(q3, q_norm_g):
    b, s, _ = q3.shape
    return pl.pallas_call(
        _q_prep_kernel,
        grid=(b, s // TQ),
        in_specs=[pl.BlockSpec((1, TQ, N_HEADS * HEAD_DIM), lambda i, j: (i, j, 0)), _full((1, HEAD_DIM))],
        out_specs=pl.BlockSpec((1, N_KV, 1, GROUP * TQ, QK_LANES), lambda i, j: (i, 0, j, 0, 0)),
        out_shape=jax.ShapeDtypeStruct((b, N_KV, s // TQ, GROUP * TQ, QK_LANES), BF16),
        compiler_params=_cparams("parallel", "parallel"),
        name="q_prep",
    )(q3, q_norm_g.reshape(1, HEAD_DIM))


def _attn_kernel(flag_ref, q_ref, gt_ref, kct_ref, vc_ref, ovt_ref, kat_ref, vs_ref, kwt_ref, vw_ref,
                 db_ref, wb_ref, ge_ref, o_ref):
    qi = pl.program_id(2)
    st = qi * TQ
    rows = GROUP * TQ
    ncp = kct_ref.shape[3]
    static_ok = flag_ref[0, 0] != 0
    both = lambda f: lax.cond(static_ok, lambda: f(True), lambda: f(False))

    qsb = q_ref[0, 0, 0]
    t_row = st + (lax.broadcasted_iota(jnp.int32, (rows, 1), 0) & (TQ - 1))

    def compressed(static):
        cidx = lax.broadcasted_iota(jnp.int32, (1, ncp), 1)
        valid = (cidx * CMP_STRIDE + (CMP_LEN - 1)) <= t_row
        sm = jnp.where(valid, _dot(qsb, kct_ref[0, 0]), NEG)
        e = _exp_terms(sm, True) if static else jnp.where(valid, _exp_terms(sm, False), 0.0)
        l = jnp.sum(e, axis=-1, keepdims=True)
        p = e * (1.0 / jnp.where(l > 0.0, l, 1.0))
        return _dot(p.astype(BF16), vc_ref[0, 0]), p[0:TQ] + p[TQ:2 * TQ] + p[2 * TQ:3 * TQ] + p[3 * TQ:4 * TQ]

    o_cmp, psum = both(compressed)

    p_hi, p_lo = _split_bf16(psum)
    imp_t = _dot_nt(ovt_ref[...], p_hi) + _dot_nt(ovt_ref[...], p_lo)
    jb = lax.broadcasted_iota(jnp.int32, (SEL_LANES, 1), 0)
    jbf = jb.astype(F32)
    cur = lax.shift_right_logical(st + lax.broadcasted_iota(jnp.int32, (1, TQ), 1), 6)
    forced = (jb == 0) | (jb == cur) | (jb == cur - 1)
    sel = forced
    work = jnp.where((jb <= cur) & jnp.logical_not(forced), imp_t, -FORCE)
    for _ in range(N_SELECT - N_FORCED):
        mx = jnp.max(work, axis=0, keepdims=True)
        first = jnp.min(jnp.where(work == mx, jbf, float(SEL_LANES)), axis=0, keepdims=True)
        pick = jbf == first
        sel = sel | pick
        work = jnp.where(pick, -jnp.inf, work)
    first_blk = lax.shift_right_logical(st, 6)

    def with_bias(keep):
        bias = jnp.where(sel & keep, 0.0, NEG).T.astype(BF16)
        return jnp.concatenate([jnp.concatenate([bias] * GROUP, axis=0), qsb], axis=1)

    qaug = with_bias(jb < first_blk)
    qaug_d = with_bias((jb >= first_blk) & (jb <= cur))

    dkeys = pl.ds(pl.multiple_of(st, TQ), TQ)
    wkeys = pl.ds(pl.multiple_of(st, TQ), WINDOW + TQ)
    n_trips = lax.shift_right_logical(st + SEL_TRIP - 1, SEL_TRIP.bit_length() - 1)

    def branches(static):
        sd = _per_head(_dot(qaug_d, kat_ref[0, 0, :, dkeys]), db_ref[...])
        vd = vs_ref[0, 0, dkeys, :]
        if static:
            acc0 = _dot(_exp_terms(sd, True).astype(BF16), vd)

            def trip(c, acc):
                keys = pl.ds(pl.multiple_of(c * SEL_TRIP, SEL_TRIP), SEL_TRIP)
                pc = jnp.exp2(_dot(qaug, kat_ref[0, 0, :, keys]))
                return acc + _dot(pc.astype(BF16), vs_ref[0, 0, keys, :])

            acc_s = lax.fori_loop(0, n_trips, trip, acc0)
        else:
            m0 = jnp.max(sd, axis=-1, keepdims=True)
            acc0 = _dot(jnp.exp2(sd - m0).astype(BF16), vd)

            def trip(c, carry):
                m, acc = carry
                keys = pl.ds(pl.multiple_of(c * SEL_TRIP, SEL_TRIP), SEL_TRIP)
                sc = _dot(qaug, kat_ref[0, 0, :, keys])
                mn = jnp.maximum(m, jnp.max(sc, axis=-1, keepdims=True))
                acc = jnp.exp2(m - mn) * acc + _dot(jnp.exp2(sc - mn).astype(BF16), vs_ref[0, 0, keys, :])
                return mn, acc

            _, acc_s = lax.fori_loop(0, n_trips, trip, (m0, acc0))
        sw = _per_head(_dot(qsb, kwt_ref[0, 0, :, wkeys]), wb_ref[...])
        acc_w = _dot(_exp_terms(sw, static).astype(BF16), vw_ref[0, 0, wkeys, :])
        return acc_s, acc_w

    acc_s, acc_w = both(branches)
    g_hi, g_lo = _split_bf16(_sigmoid(gt_ref[0]))
    out = jnp.zeros((TQ, KV_W), F32)
    for j, y in enumerate((o_cmp, _normalise(acc_s), _normalise(acc_w))):
        out = out + (_dot(g_hi, ge_ref[j]) + _dot(g_lo, ge_ref[j])) * _assemble(y)
    o_ref[0] = out


def _attention(flag, qpad, gates3, kcmp_t, vcmp, overlap_t, kaug_t, vslc, kwin_t, vwin):
    b, _, n_tiles, rows, _ = qpad.shape
    s = n_tiles * TQ
    ncp = kcmp_t.shape[3]
    sp = kwin_t.shape[3]
    per_bg = lambda shape: pl.BlockSpec((1, 1) + shape, lambda i, j, k: (i, j, 0, 0))
    return pl.pallas_call(
        _attn_kernel,
        grid=(b, N_KV, n_tiles),
        in_specs=[pl.BlockSpec(memory_space=pltpu.SMEM),
                  pl.BlockSpec((1, 1, 1, rows, QK_LANES), lambda i, j, k: (i, j, k, 0, 0)),
                  pl.BlockSpec((1, TQ, GATE_LANES), lambda i, j, k: (i, k, j)),
                  per_bg((QK_LANES, ncp)), per_bg((ncp, 2 * HEAD_DIM)), _full((SEL_LANES, ncp)),
                  per_bg((SEL_LANES + QK_LANES, s)), per_bg((s, V_LANES)),
                  per_bg((QK_LANES, sp)), per_bg((sp, V_LANES)),
                  _full((TQ, TQ)), _full((TQ, WINDOW + TQ)), _full((3, GATE_LANES, KV_W))],
        out_specs=pl.BlockSpec((1, TQ, KV_W), lambda i, j, k: (i, k, j)),
        out_shape=jax.ShapeDtypeStruct((b, s, N_HEADS * HEAD_DIM), F32),
        compiler_params=_cparams("parallel", "parallel", "arbitrary"),
        name="nsa_attention",
    )(flag, qpad, gates3, kcmp_t, vcmp, overlap_t, kaug_t, vslc, kwin_t, vwin, *_band_biases(),
      _gate_expand())


def _gate_expand():
    e = np.zeros((3, GATE_LANES, KV_W), np.float32)
    for r in range(GROUP):
        for j in range(3):
            e[j, 3 * r + j, r * HEAD_DIM:(r + 1) * HEAD_DIM] = 1.0
    return jnp.asarray(e, dtype=BF16)


def _band_biases():
    q = np.arange(TQ)[:, None]
    causal = np.where(np.arange(TQ)[None, :] <= q, 0.0, NEG)
    kw = np.arange(WINDOW + TQ)[None, :] - WINDOW
    band = np.where((kw <= q) & (kw > q - WINDOW), 0.0, NEG)
    return jnp.asarray(causal, F32), jnp.asarray(band, F32)


def _overlap_matrix_t(s):
    ncp = s // CMP_STRIDE
    i = np.arange(ncp)[None, :]
    j = np.arange(SEL_LANES)[:, None]
    lo = np.maximum(i * CMP_STRIDE, j * SEL_LEN)
    hi = np.minimum(i * CMP_STRIDE + CMP_LEN, (j + 1) * SEL_LEN)
    ov = np.maximum(hi - lo, 0) / CMP_STRIDE
    ov = ov * (i < (s - CMP_LEN) // CMP_STRIDE + 1) * (j < s // SEL_LEN)
    return jnp.asarray(ov, dtype=BF16)


def _block_onehot_t(s):
    oh = (np.arange(SEL_LANES)[:, None] == (np.arange(s)[None, :] // SEL_LEN)).astype(np.float32)
    return jnp.asarray(oh, dtype=BF16)


def _static_flag(q_g, *k_gs):
    kmax = functools.reduce(jnp.maximum, [jnp.max(jnp.abs(g)) for g in k_gs])
    bound = HEAD_DIM * SCALE * jnp.max(jnp.abs(q_g)) * kmax
    return (bound < STATIC_SCORE_BOUND).astype(jnp.int32).reshape(1, 1)


def _merge_kernel(x_ref, yr_ref, o_ref, mg_ref, wa_ref, wo_ref, g2_ref, x1_ref, hn_ref):
    y_attn = _dot(o_ref[...].astype(BF16), wa_ref[...])
    mg = mg_ref[...]
    m = _sigmoid(mg[:, :D_MODEL]) * yr_ref[...] + _sigmoid(mg[:, D_MODEL:]) * y_attn
    x1 = x_ref[...] + _dot(m.astype(BF16), wo_ref[...])
    x1_ref[...] = x1
    hn_ref[...] = _rms(x1, g2_ref[...]).astype(BF16)


def _merge(x2, y_rnn, o_attn, mg, w_attn_out, w_out, norm2_g):
    t = x2.shape[0]
    tm = _tile(t, 512)
    rowblk = lambda n: pl.BlockSpec((tm, n), lambda i: (i, 0))
    return pl.pallas_call(
        _merge_kernel,
        grid=(t // tm,),
        in_specs=[rowblk(D_MODEL), rowblk(D_MODEL), rowblk(D_MODEL), rowblk(2 * D_MODEL),
                  _full((D_MODEL, D_MODEL)), _full((D_MODEL, D_MODEL)), _full((1, D_MODEL))],
        out_specs=[rowblk(D_MODEL), rowblk(D_MODEL)],
        out_shape=[jax.ShapeDtypeStruct((t, D_MODEL), F32), jax.ShapeDtypeStruct((t, D_MODEL), BF16)],
        compiler_params=_cparams("parallel"),
        name="merge",
    )(x2, y_rnn, o_attn, mg, w_attn_out.astype(BF16), w_out.astype(BF16), norm2_g.reshape(1, D_MODEL))


def _ffn_up_kernel(hp_ref, h_ref, wg_ref, wv_ref, cwg_ref, cwv_ref, cbg_ref, cbv_ref, act_ref, up_ref,
                   *, tm, tiles_per_seq):
    first = (pl.program_id(0) % tiles_per_seq) == 0
    keep = jnp.where(first, 0.0, 1.0)

    def conv(w_ref, cw_ref, cb_ref):
        up_ref[0:HALO, :] = _dot(hp_ref[...], w_ref[...]) * keep
        up_ref[HALO:HALO + tm, :] = _dot(h_ref[...], w_ref[...])
        cw = cw_ref[...]
        y = cb_ref[...] + cw[FFN_CONV_W - 1:FFN_CONV_W, :] * up_ref[HALO:HALO + tm, :]
        for k in range(FFN_CONV_W - 1):
            off = HALO - (FFN_CONV_W - 1) + k
            y = y + cw[k:k + 1, :] * up_ref[off:off + tm, :]
        return y

    gate = conv(wg_ref, cwg_ref, cbg_ref)
    val = conv(wv_ref, cwv_ref, cbv_ref)
    act_ref[...] = (_gelu(gate) * val).astype(BF16)


def _ffn_up(hn, w_up, conv_w, conv_b, s):
    t = hn.shape[0]
    tm = _tile(s, 1024)
    tn = 512
    nj = D_FF // tn
    wb = w_up.astype(BF16)
    cb = conv_b.reshape(1, 2 * D_FF)
    return pl.pallas_call(
        functools.partial(_ffn_up_kernel, tm=tm, tiles_per_seq=s // tm),
        grid=(t // tm, nj),
        in_specs=[pl.BlockSpec((HALO, D_MODEL), lambda i, j: (jnp.maximum(i * (tm // HALO) - 1, 0), 0)),
                  pl.BlockSpec((tm, D_MODEL), lambda i, j: (i, 0)),
                  pl.BlockSpec((D_MODEL, tn), lambda i, j: (0, j)),
                  pl.BlockSpec((D_MODEL, tn), lambda i, j: (0, j + nj)),
                  pl.BlockSpec((FFN_CONV_W, tn), lambda i, j: (0, j)),
                  pl.BlockSpec((FFN_CONV_W, tn), lambda i, j: (0, j + nj)),
                  pl.BlockSpec((1, tn), lambda i, j: (0, j)),
                  pl.BlockSpec((1, tn), lambda i, j: (0, j + nj))],
        out_specs=pl.BlockSpec((tm, tn), lambda i, j: (i, j)),
        out_shape=jax.ShapeDtypeStruct((t, D_FF), BF16),
        scratch_shapes=[pltpu.VMEM((tm + HALO, tn), F32)],
        compiler_params=_cparams("parallel", "arbitrary"),
        name="ffn_up",
    )(hn, hn, wb, wb, conv_w, conv_w, cb, cb)


def _ffn_down_kernel(x1_ref, act_ref, w_ref, o_ref):
    o_ref[...] = x1_ref[...] + _dot(act_ref[...], w_ref[...])


def _ffn_down(x1, act, w_down):
    t = x1.shape[0]
    tm = _tile(t, 512)
    return pl.pallas_call(
        _ffn_down_kernel,
        grid=(t // tm,),
        in_specs=[pl.BlockSpec((tm, D_MODEL), lambda i: (i, 0)), pl.BlockSpec((tm, D_FF), lambda i: (i, 0)),
                  _full((D_FF, D_MODEL))],
        out_specs=pl.BlockSpec((tm, D_MODEL), lambda i: (i, 0)),
        out_shape=jax.ShapeDtypeStruct((t, D_MODEL), F32),
        compiler_params=_cparams("parallel"),
        name="ffn_down",
    )(x1, act, w_down.astype(BF16))


def _heads_t(a, b, s):
    return a.reshape(b, s, N_KV, -1).transpose(0, 2, 3, 1)


def _heads(a, b, s):
    return a.reshape(b, s, N_KV, -1).transpose(0, 2, 1, 3)


def _cmp_rows(a, b, s):
    a = a.reshape(b, s // CMP_STRIDE, CMP_STRIDE, N_KV, HEAD_DIM).transpose(0, 3, 1, 2, 4)
    return a.reshape(b, N_KV, s // CMP_STRIDE, CMP_STRIDE * HEAD_DIM)


def kernel(x, norm1_g, w_in, rnn_conv_w, rnn_conv_b, rg_w_a, rg_b_a, rg_w_x, rg_b_x, lru_lambda, w_rnn_out, q_norm_g, k_cmp_norm_g, k_slc_norm_g, k_win_norm_g, cmp_pe_k, cmp_w1_k, cmp_w2_k, cmp_pe_v, cmp_w1_v, cmp_w2_v, w_attn_out, w_out, norm2_g, w_up, ffn_conv_w, ffn_conv_b, w_down):
    b, s, d = x.shape
    assert d == D_MODEL and s % SEL_TRIP == 0 and s // SEL_LEN <= SEL_LANES
    t = b * s
    x2 = x.reshape(t, d)

    rx, rgate, q, kv, gates, mg = _in_proj(x2, norm1_g, w_in)

    y_rnn = _rglru(rx.reshape(b, s, D_RNN), rgate.reshape(b, s, D_RNN), rnn_conv_w, rnn_conv_b,
                   rg_w_a, rg_b_a, rg_w_x, rg_b_x, lru_lambda, w_rnn_out).reshape(t, d)

    ks, vs, kw, vw = _kv_prep(kv, k_slc_norm_g, k_win_norm_g)
    kcmp, vcmp = _compress(_cmp_rows(kv[:, :KV_W], b, s), _cmp_rows(kv[:, KV_W:2 * KV_W], b, s),
                           cmp_pe_k, cmp_pe_v, cmp_w1_k, cmp_w1_v, cmp_w2_k, cmp_w2_v, k_cmp_norm_g)

    kpad = ((0, 0), (0, 0), (0, QK_LANES - HEAD_DIM), (0, 0))
    onehot = jnp.broadcast_to(_block_onehot_t(s), (b, N_KV, SEL_LANES, s))
    kaug_t = jnp.concatenate([onehot, jnp.pad(_heads_t(ks, b, s), kpad)], axis=2)
    pad_row = jnp.where(jnp.arange(s + WINDOW) < WINDOW, NEG, 0.0).astype(BF16)
    kwin_t = jnp.concatenate(
        [jnp.pad(_heads_t(kw, b, s), ((0, 0), (0, 0), (0, 0), (WINDOW, 0))),
         jnp.broadcast_to(pad_row, (b, N_KV, 1, s + WINDOW)),
         jnp.zeros((b, N_KV, QK_LANES - HEAD_DIM - 1, s + WINDOW), BF16)], axis=2)
    vwin = jnp.pad(_heads(vw, b, s), ((0, 0), (0, 0), (WINDOW, 0), (0, 0)))
    kcmp_t = jnp.pad(kcmp.transpose(0, 1, 3, 2).astype(BF16), kpad)
    flag = _static_flag(q_norm_g, k_cmp_norm_g, k_slc_norm_g, k_win_norm_g)
    qpad = _q_prep(q.reshape(b, s, d), q_norm_g)
    vcmp2 = jnp.concatenate([vcmp, vcmp], axis=-1).astype(BF16)
    o_attn = _attention(flag, qpad, gates.reshape(b, s, N_KV * GATE_LANES), kcmp_t, vcmp2,
                        _overlap_matrix_t(s), kaug_t, _heads(vs, b, s), kwin_t, vwin).reshape(t, d)

    x1, hn = _merge(x2, y_rnn, o_attn, mg, w_attn_out, w_out, norm2_g)
    act = _ffn_up(hn, w_up, ffn_conv_w, ffn_conv_b, s)
    return _ffn_down(x1, act, w_down).reshape(b, s, d)
```

```python
import functools

import numpy as np
import jax
import jax.numpy as jnp
from jax import lax
from jax.experimental import pallas as pl
from jax.experimental.pallas import tpu as pltpu

F32 = jnp.float32
BF16 = jnp.bfloat16

D_MODEL = 1024
D_RNN = 1344
RNN_BLOCKS = 16
RNN_CONV_W = 4
LRU_C = 8.0
N_HEADS = 16
N_KV = 4
HEAD_DIM = 64
GROUP = N_HEADS // N_KV
KV_W = N_KV * HEAD_DIM
CMP_LEN = 32
CMP_STRIDE = 16
CMP_HIDDEN = 128
SEL_LEN = 64
N_SELECT = 16
N_FORCED = 3
WINDOW = 512
D_FF = 3 * D_MODEL
FFN_CONV_W = 3
EPS = 1e-6
NEG = -1e30
FORCE = 1e6
SCALE = HEAD_DIM ** -0.5

SEL_LANES = 128
V_LANES = 128
QK_LANES = 128
GATE_LANES = 128
TQ = 256
SEL_TRIP = 1024
LOG2E = float(np.log2(np.e))
HALO = 8
VMEM_LIMIT = 56 * 1024 * 1024
STATIC_SCORE_BOUND = 40.0


def _cparams(*sem):
    return pltpu.CompilerParams(dimension_semantics=sem, vmem_limit_bytes=VMEM_LIMIT)


def _gelu(x):
    c = np.float32(np.sqrt(2.0 / np.pi))
    return x * (0.5 * (1.0 + jnp.tanh(c * (x + 0.044715 * (x * x * x)))))


def _sigmoid(x):
    return 0.5 + 0.5 * jnp.tanh(0.5 * x)


def _rms(x, g):
    ms = jnp.mean(x * x, axis=-1, keepdims=True)
    return x * lax.rsqrt(ms + EPS) * g


def _dot(a, b):
    return jnp.dot(a, b, preferred_element_type=F32)


def _dot_nt(a, b):
    return lax.dot_general(a, b, (((1,), (1,)), ((), ())), preferred_element_type=F32)


def _split_bf16(x):
    hi = x.astype(BF16)
    lo = (x - hi.astype(F32)).astype(BF16)
    return hi, lo


def _tile(n, want):
    t = min(n, want)
    while n % t:
        t //= 2
    return t


def _full(shape):
    nd = len(shape)
    return pl.BlockSpec(shape, lambda *_: (0,) * nd)


def _in_proj_kernel(x_ref, g_ref, w_rx, w_rg, w_q, w_kv, w_g, w_mg, qg_ref, gs_ref, gw_ref, eye_ref,
                    kw0_ref, vw0_ref,
                    o_rx, o_rg, o_cmp, o_g, o_mg, q_ref, ka_ref, vs_ref, kw_ref, vw_ref, *, tiles_per_seq):
    del kw0_ref, vw0_ref
    h = _rms(x_ref[...], g_ref[...]).astype(BF16)

    q = _dot(h, w_q[...])
    kv = _dot(h, w_kv[...])
    o_rx[...] = _dot(h, w_rx[...])
    zeros = jnp.zeros((TQ, QK_LANES - HEAD_DIM), BF16)
    for g in range(N_KV):
        for r in range(GROUP):
            lo = (g * GROUP + r) * HEAD_DIM
            y = _rms(q[:, lo:lo + HEAD_DIM], qg_ref[...]) * (SCALE * LOG2E)
            q_ref[0, g, 0, r * TQ:(r + 1) * TQ, 0:HEAD_DIM] = y.astype(BF16)
            q_ref[0, g, 0, r * TQ:(r + 1) * TQ, HEAD_DIM:QK_LANES] = zeros

    o_rg[...] = _dot(h, w_rg[...])
    o_cmp[...] = kv[:, :2 * KV_W]
    pos = (pl.program_id(0) % tiles_per_seq) * TQ + lax.broadcasted_iota(jnp.int32, (1, TQ), 1)
    blk = lax.shift_right_logical(pos, SEL_LEN.bit_length() - 1)
    onehot = (lax.broadcasted_iota(jnp.int32, (SEL_LANES, 1), 0) == blk).astype(BF16)
    ones = jnp.ones((TQ, V_LANES - HEAD_DIM), BF16)
    for g in range(N_KV):
        lo = g * HEAD_DIM
        ks = _rms(kv[:, 2 * KV_W + lo:2 * KV_W + lo + HEAD_DIM], gs_ref[...]).astype(BF16)
        kw = _rms(kv[:, 4 * KV_W + lo:4 * KV_W + lo + HEAD_DIM], gw_ref[...]).astype(BF16)
        ka_ref[0, g, 0:SEL_LANES, :] = onehot
        ka_ref[0, g, SEL_LANES:SEL_LANES + QK_LANES, :] = _dot_nt(eye_ref[...], ks).astype(BF16)
        kw_ref[0, g] = _dot_nt(eye_ref[...], kw).astype(BF16)
        for src, dst in ((3, vs_ref), (5, vw_ref)):
            dst[0, g, :, 0:HEAD_DIM] = kv[:, src * KV_W + lo:src * KV_W + lo + HEAD_DIM].astype(BF16)
            dst[0, g, :, HEAD_DIM:V_LANES] = ones

    o_g[...] = _dot(h, w_g[...])
    o_mg[...] = _dot(h, w_mg[...])


def _in_proj(x2, b, s, norm1_g, w_in, q_g, g_slc, g_win):
    t = x2.shape[0]
    nq = s // TQ
    pad_blocks = WINDOW // TQ
    kvw = 6 * KV_W
    o0 = 2 * D_RNN
    o1 = o0 + N_HEADS * HEAD_DIM
    o2 = o1 + kvw
    o3 = o2 + 3 * N_HEADS
    wb = w_in.astype(BF16)
    w_g = jnp.pad(wb[:, o2:o3].reshape(D_MODEL, N_KV, 3 * GROUP), ((0, 0), (0, 0), (0, GATE_LANES - 3 * GROUP)))
    w_g = w_g.reshape(D_MODEL, N_KV * GATE_LANES)
    ws = [wb[:, :D_RNN], wb[:, D_RNN:o0], wb[:, o0:o1], wb[:, o1:o2], w_g, wb[:, o3:]]
    eye = jnp.asarray(np.eye(QK_LANES, HEAD_DIM), BF16)
    gain = lambda v: v.reshape(1, HEAD_DIM)
    rows = lambda n: pl.BlockSpec((TQ, n), lambda i: (i, 0))
    flat = lambda n: jax.ShapeDtypeStruct((t, n), F32)
    kwin0 = jnp.zeros((b, N_KV, QK_LANES, s + WINDOW), BF16)
    vwin0 = jnp.zeros((b, N_KV, s + WINDOW, V_LANES), BF16)
    n_in = 14
    return pl.pallas_call(
        functools.partial(_in_proj_kernel, tiles_per_seq=nq),
        grid=(t // TQ,),
        in_specs=[rows(D_MODEL), _full((1, D_MODEL))] + [_full(w.shape) for w in ws]
        + [_full((1, HEAD_DIM))] * 3 + [_full((QK_LANES, HEAD_DIM))]
        + [pl.BlockSpec(memory_space=pl.ANY)] * 2,
        out_specs=[rows(D_RNN), rows(D_RNN), rows(2 * KV_W), rows(N_KV * GATE_LANES), rows(2 * D_MODEL),
                   pl.BlockSpec((1, N_KV, 1, GROUP * TQ, QK_LANES), lambda i: (i // nq, 0, i % nq, 0, 0)),
                   pl.BlockSpec((1, N_KV, SEL_LANES + QK_LANES, TQ), lambda i: (i // nq, 0, 0, i % nq)),
                   pl.BlockSpec((1, N_KV, TQ, V_LANES), lambda i: (i // nq, 0, i % nq, 0)),
                   pl.BlockSpec((1, N_KV, QK_LANES, TQ), lambda i: (i // nq, 0, 0, i % nq + pad_blocks)),
                   pl.BlockSpec((1, N_KV, TQ, V_LANES), lambda i: (i // nq, 0, i % nq + pad_blocks, 0))],
        out_shape=[flat(D_RNN), flat(D_RNN), flat(2 * KV_W), flat(N_KV * GATE_LANES), flat(2 * D_MODEL),
                   jax.ShapeDtypeStruct((b, N_KV, nq, GROUP * TQ, QK_LANES), BF16),
                   jax.ShapeDtypeStruct((b, N_KV, SEL_LANES + QK_LANES, s), BF16),
                   jax.ShapeDtypeStruct((b, N_KV, s, V_LANES), BF16),
                   jax.ShapeDtypeStruct(kwin0.shape, BF16), jax.ShapeDtypeStruct(vwin0.shape, BF16)],
        input_output_aliases={n_in - 2: 8, n_in - 1: 9},
        compiler_params=_cparams("parallel"),
        name="in_proj",
    )(x2, norm1_g.reshape(1, D_MODEL), *ws, gain(q_g), gain(g_slc), gain(g_win), eye, kwin0, vwin0)


def _rglru_kernel(rx_ref, rg_ref, cw_ref, cb_ref, wa_ref, ba_ref, wx_ref, bx_ref, lam_ref, wo_ref,
                  y_ref, xs_ref, a_ref, b_ref, carry_ref, *, tt):
    ti = pl.program_id(1)

    @pl.when(ti == 0)
    def _():
        xs_ref[0:HALO, :] = jnp.zeros((HALO, D_RNN), F32)
        carry_ref[...] = jnp.zeros((HALO, D_RNN), F32)

    xs_ref[HALO:HALO + tt, :] = rx_ref[0]
    cw = cw_ref[...]
    x = cb_ref[...] + cw[RNN_CONV_W - 1:RNN_CONV_W, :] * xs_ref[HALO:HALO + tt, :]
    for k in range(RNN_CONV_W - 1):
        off = HALO - (RNN_CONV_W - 1) + k
        x = x + cw[k:k + 1, :] * xs_ref[off:off + tt, :]
    xs_ref[0:HALO, :] = xs_ref[tt:tt + HALO, :]

    xb = x.astype(BF16)
    r = _sigmoid(_dot(xb, wa_ref[...]) + ba_ref[...])
    gi = _sigmoid(_dot(xb, wx_ref[...]) + bx_ref[...])
    lam = lam_ref[...]
    log_sig = jnp.minimum(lam, 0.0) - jnp.log1p(jnp.exp(-jnp.abs(lam)))
    log_a = LRU_C * r * log_sig
    a = jnp.exp(log_a)
    y = 1.0 - a * a
    u = jnp.where(y > 0.0, y * lax.rsqrt(y), 0.0) * (gi * x)

    sub = lax.broadcasted_iota(jnp.int32, (tt, 1), 0) & (HALO - 1)
    for d in (1, 2, 4):
        keep = sub >= d
        a_sh = pltpu.roll(a, d, 0)
        u_sh = pltpu.roll(u, d, 0)
        u = jnp.where(keep, a * u_sh + u, u)
        a = jnp.where(keep, a * a_sh, a)
    a_ref[...] = a
    b_ref[...] = u

    def group(gidx, carry):
        rows = pl.ds(pl.multiple_of(gidx * HALO, HALO), HALO)
        h = a_ref[rows, :] * carry + b_ref[rows, :]
        b_ref[rows, :] = h
        return jnp.broadcast_to(h[HALO - 1:HALO, :], (HALO, D_RNN))

    carry_ref[...] = lax.fori_loop(0, tt // HALO, group, carry_ref[...], unroll=4)

    hg = (b_ref[...] * _gelu(rg_ref[0])).astype(BF16)
    y_ref[0] = _dot(hg, wo_ref[...])


def _block_diag(w):
    n, c, d = w.shape
    return jnp.einsum("ncd,nm->ncmd", w, jnp.eye(n, dtype=w.dtype)).reshape(n * c, n * d)


def _rglru(rx3, rg3, conv_w, conv_b, w_a, b_a, w_x, b_x, lam, w_out):
    b, s, _ = rx3.shape
    tt = _tile(s, 256)
    row = lambda v: v.reshape(1, D_RNN)
    return pl.pallas_call(
        functools.partial(_rglru_kernel, tt=tt),
        grid=(b, s // tt),
        in_specs=[pl.BlockSpec((1, tt, D_RNN), lambda i, j: (i, j, 0)),
                  pl.BlockSpec((1, tt, D_RNN), lambda i, j: (i, j, 0)),
                  _full((RNN_CONV_W, D_RNN)), _full((1, D_RNN)),
                  _full((D_RNN, D_RNN)), _full((1, D_RNN)),
                  _full((D_RNN, D_RNN)), _full((1, D_RNN)),
                  _full((1, D_RNN)), _full((D_RNN, D_MODEL))],
        out_specs=pl.BlockSpec((1, tt, D_MODEL), lambda i, j: (i, j, 0)),
        out_shape=jax.ShapeDtypeStruct((b, s, D_MODEL), F32),
        scratch_shapes=[pltpu.VMEM((tt + HALO, D_RNN), F32), pltpu.VMEM((tt, D_RNN), F32),
                        pltpu.VMEM((tt, D_RNN), F32), pltpu.VMEM((HALO, D_RNN), F32)],
        compiler_params=_cparams("arbitrary", "arbitrary"),
        name="rglru",
    )(rx3, rg3, conv_w, row(conv_b), _block_diag(w_a).astype(BF16), row(b_a),
      _block_diag(w_x).astype(BF16), row(b_x), row(lam), w_out.astype(BF16))


def _compress_kernel(rk_ref, rv_ref, pek_ref, pev_ref, w1k_ref, w1v_ref, w2k_ref, w2v_ref, gk_ref,
                     kc_ref, vc_ref):
    half = CMP_STRIDE * HEAD_DIM
    ncp = rk_ref.shape[2]

    def mlp(r_ref, pe_ref, w1_ref, w2_ref):
        r = r_ref[0, 0]
        pe = pe_ref[...]
        lo = _dot((r + pe[:, :half]).astype(BF16), w1_ref[:half, :])
        hi = _dot((r + pe[:, half:]).astype(BF16), w1_ref[half:, :])
        hid = lo + pltpu.roll(hi, ncp - 1, 0)
        return _dot(_gelu(hid).astype(BF16), w2_ref[...])

    kc_ref[0, 0] = _rms(mlp(rk_ref, pek_ref, w1k_ref, w2k_ref), gk_ref[...])
    vc_ref[0, 0] = mlp(rv_ref, pev_ref, w1v_ref, w2v_ref)


def _compress(rk, rv, pe_k, pe_v, w1k, w1v, w2k, w2v, g_cmp):
    b, g, ncp, wide = rk.shape
    blk = pl.BlockSpec((1, 1, ncp, wide), lambda i, j: (i, j, 0, 0))
    out = pl.BlockSpec((1, 1, ncp, HEAD_DIM), lambda i, j: (i, j, 0, 0))
    w1 = _full((CMP_LEN * HEAD_DIM, CMP_HIDDEN))
    w2 = _full((CMP_HIDDEN, HEAD_DIM))
    pe = _full((1, CMP_LEN * HEAD_DIM))
    return pl.pallas_call(
        _compress_kernel,
        grid=(b, g),
        in_specs=[blk, blk, pe, pe, w1, w1, w2, w2, _full((1, HEAD_DIM))],
        out_specs=[out, out],
        out_shape=[jax.ShapeDtypeStruct((b, g, ncp, HEAD_DIM), F32)] * 2,
        compiler_params=_cparams("parallel", "parallel"),
        name="compress",
    )(rk, rv, pe_k.reshape(1, -1), pe_v.reshape(1, -1), w1k.astype(BF16), w1v.astype(BF16),
      w2k.astype(BF16), w2v.astype(BF16), g_cmp.reshape(1, HEAD_DIM))


def _exp_terms(sm, static):
    if static:
        return jnp.exp2(sm)
    return jnp.exp2(sm - jnp.max(sm, axis=-1, keepdims=True))


def _per_head(s, bias):
    return (s.reshape(GROUP, TQ, s.shape[-1]) + bias[None]).reshape(s.shape)


def _normalise(acc):
    swapped = pltpu.roll(acc, HEAD_DIM, 1)
    parts = []
    for r in range(GROUP):
        a, b = acc[r * TQ:(r + 1) * TQ], swapped[r * TQ:(r + 1) * TQ]
        parts.append(a * (1.0 / b) if r % 2 == 0 else b * (1.0 / a))
    return jnp.concatenate(parts, axis=0)


def _assemble(y):
    low = lax.broadcasted_iota(jnp.int32, (1, 2 * HEAD_DIM), 1) < HEAD_DIM
    pair = lambda r: jnp.where(low, y[r * TQ:(r + 1) * TQ], y[(r + 1) * TQ:(r + 2) * TQ])
    return jnp.concatenate([pair(0), pair(2)], axis=1)


def _attn_kernel(flag_ref, q_ref, gt_ref, kct_ref, vc_ref, ovt_ref, kat_ref, vs_ref, kwt_ref, vw_ref,
                 db_ref, wb_ref, ge_ref, o_ref):
    qi = pl.program_id(2)
    st = qi * TQ
    rows = GROUP * TQ
    ncp = kct_ref.shape[3]
    static_ok = flag_ref[0, 0] != 0
    both = lambda f: lax.cond(static_ok, lambda: f(True), lambda: f(False))

    qsb = q_ref[0, 0, 0]
    t_row = st + (lax.broadcasted_iota(jnp.int32, (rows, 1), 0) & (TQ - 1))

    def compressed(static):
        def over(n):
            cidx = lax.broadcasted_iota(jnp.int32, (1, n), 1)
            valid = (cidx * CMP_STRIDE + (CMP_LEN - 1)) <= t_row
            sm = jnp.where(valid, _dot(qsb, kct_ref[0, 0, :, 0:n]), NEG)
            e = _exp_terms(sm, True) if static else jnp.where(valid, _exp_terms(sm, False), 0.0)
            l = jnp.sum(e, axis=-1, keepdims=True)
            p = e * (1.0 / jnp.where(l > 0.0, l, 1.0))
            psum = p[0:TQ] + p[TQ:2 * TQ] + p[2 * TQ:3 * TQ] + p[3 * TQ:4 * TQ]
            if n < ncp:
                psum = jnp.concatenate([psum, jnp.zeros((TQ, ncp - n), F32)], axis=1)
            return _dot(p.astype(BF16), vc_ref[0, 0, 0:n, :]), psum

        half = ncp // 2
        if half % 128:
            return over(ncp)
        return lax.cond(st + TQ <= half * CMP_STRIDE + CMP_LEN - 1, lambda: over(half), lambda: over(ncp))

    o_cmp, psum = both(compressed)

    p_hi, p_lo = _split_bf16(psum)
    imp_t = _dot_nt(ovt_ref[...], p_hi) + _dot_nt(ovt_ref[...], p_lo)
    jb = lax.broadcasted_iota(jnp.int32, (SEL_LANES, 1), 0)
    jbf = jb.astype(F32)
    cur = lax.shift_right_logical(st + lax.broadcasted_iota(jnp.int32, (1, TQ), 1), 6)
    forced = (jb == 0) | (jb == cur) | (jb == cur - 1)
    sel = forced
    work = jnp.where((jb <= cur) & jnp.logical_not(forced), imp_t, -FORCE)
    for _ in range(N_SELECT - N_FORCED):
        mx = jnp.max(work, axis=0, keepdims=True)
        first = jnp.min(jnp.where(work == mx, jbf, float(SEL_LANES)), axis=0, keepdims=True)
        pick = jbf == first
        sel = sel | pick
        work = jnp.where(pick, -jnp.inf, work)
    first_blk = lax.shift_right_logical(st, 6)

    def with_bias(keep):
        bias = jnp.where(sel & keep, 0.0, NEG).T.astype(BF16)
        return jnp.concatenate([jnp.concatenate([bias] * GROUP, axis=0), qsb], axis=1)

    qaug = with_bias(jb < first_blk)
    qaug_d = with_bias((jb >= first_blk) & (jb <= cur))

    dkeys = pl.ds(pl.multiple_of(st, TQ), TQ)
    wkeys = pl.ds(pl.multiple_of(st, TQ), WINDOW + TQ)
    n_trips = lax.shift_right_logical(st + SEL_TRIP - 1, SEL_TRIP.bit_length() - 1)

    def branches(static):
        sd = _per_head(_dot(qaug_d, kat_ref[0, 0, :, dkeys]), db_ref[...])
        vd = vs_ref[0, 0, dkeys, :]
        if static:
            acc0 = _dot(_exp_terms(sd, True).astype(BF16), vd)

            def trip(c, acc):
                keys = pl.ds(pl.multiple_of(c * SEL_TRIP, SEL_TRIP), SEL_TRIP)
                pc = jnp.exp2(_dot(qaug, kat_ref[0, 0, :, keys]))
                return acc + _dot(pc.astype(BF16), vs_ref[0, 0, keys, :])

            acc_s = lax.fori_loop(0, n_trips, trip, acc0)
        else:
            m0 = jnp.max(sd, axis=-1, keepdims=True)
            acc0 = _dot(jnp.exp2(sd - m0).astype(BF16), vd)

            def trip(c, carry):
                m, acc = carry
                keys = pl.ds(pl.multiple_of(c * SEL_TRIP, SEL_TRIP), SEL_TRIP)
                sc = _dot(qaug, kat_ref[0, 0, :, keys])
                mn = jnp.maximum(m, jnp.max(sc, axis=-1, keepdims=True))
                acc = jnp.exp2(m - mn) * acc + _dot(jnp.exp2(sc - mn).astype(BF16), vs_ref[0, 0, keys, :])
                return mn, acc

            _, acc_s = lax.fori_loop(0, n_trips, trip, (m0, acc0))
        sw = _per_head(_dot(qsb, kwt_ref[0, 0, :, wkeys]), wb_ref[0])
        acc_w = _dot(_exp_terms(sw, static).astype(BF16), vw_ref[0, 0, wkeys, :])
        return acc_s, acc_w

    acc_s, acc_w = both(branches)
    g_hi, g_lo = _split_bf16(_sigmoid(gt_ref[0]))
    out = jnp.zeros((TQ, KV_W), F32)
    for j, y in enumerate((o_cmp, _normalise(acc_s), _normalise(acc_w))):
        out = out + (_dot(g_hi, ge_ref[j]) + _dot(g_lo, ge_ref[j])) * _assemble(y)
    o_ref[0] = out


def _attention(flag, qpad, gates3, kcmp_t, vcmp, overlap_t, kaug_t, vslc, kwin_t, vwin):
    b, _, n_tiles, rows, _ = qpad.shape
    s = n_tiles * TQ
    ncp = kcmp_t.shape[3]
    sp = kwin_t.shape[3]
    per_bg = lambda shape: pl.BlockSpec((1, 1) + shape, lambda i, j, k: (i, j, 0, 0))
    return pl.pallas_call(
        _attn_kernel,
        grid=(b, N_KV, n_tiles),
        in_specs=[pl.BlockSpec(memory_space=pltpu.SMEM),
                  pl.BlockSpec((1, 1, 1, rows, QK_LANES), lambda i, j, k: (i, j, k, 0, 0)),
                  pl.BlockSpec((1, TQ, GATE_LANES), lambda i, j, k: (i, k, j)),
                  per_bg((QK_LANES, ncp)), per_bg((ncp, 2 * HEAD_DIM)), _full((SEL_LANES, ncp)),
                  per_bg((SEL_LANES + QK_LANES, s)), per_bg((s, V_LANES)),
                  per_bg((QK_LANES, sp)), per_bg((sp, V_LANES)),
                  _full((TQ, TQ)),
                  pl.BlockSpec((1, TQ, WINDOW + TQ), lambda i, j, k: (jnp.minimum(k, WINDOW // TQ), 0, 0)),
                  _full((3, GATE_LANES, KV_W))],
        out_specs=pl.BlockSpec((1, TQ, KV_W), lambda i, j, k: (i, k, j)),
        out_shape=jax.ShapeDtypeStruct((b, s, N_HEADS * HEAD_DIM), F32),
        compiler_params=_cparams("parallel", "parallel", "arbitrary"),
        name="nsa_attention",
    )(flag, qpad, gates3, kcmp_t, vcmp, overlap_t, kaug_t, vslc, kwin_t, vwin, *_band_biases(),
      _gate_expand())


def _gate_expand():
    e = np.zeros((3, GATE_LANES, KV_W), np.float32)
    for r in range(GROUP):
        for j in range(3):
            e[j, 3 * r + j, r * HEAD_DIM:(r + 1) * HEAD_DIM] = 1.0
    return jnp.asarray(e, dtype=BF16)


def _band_biases():
    q = np.arange(TQ)[:, None]
    causal = np.where(np.arange(TQ)[None, :] <= q, 0.0, NEG)
    kw = np.arange(WINDOW + TQ)[None, :] - WINDOW
    in_band = (kw <= q) & (kw > q - WINDOW)
    bands = [np.where(in_band & (kw + tile * TQ >= 0), 0.0, NEG) for tile in range(WINDOW // TQ + 1)]
    return jnp.asarray(causal, F32), jnp.asarray(np.stack(bands), F32)


def _overlap_matrix_t(s):
    ncp = s // CMP_STRIDE
    i = np.arange(ncp)[None, :]
    j = np.arange(SEL_LANES)[:, None]
    lo = np.maximum(i * CMP_STRIDE, j * SEL_LEN)
    hi = np.minimum(i * CMP_STRIDE + CMP_LEN, (j + 1) * SEL_LEN)
    ov = np.maximum(hi - lo, 0) / CMP_STRIDE
    ov = ov * (i < (s - CMP_LEN) // CMP_STRIDE + 1) * (j < s // SEL_LEN)
    return jnp.asarray(ov, dtype=BF16)


def _static_flag(q_g, *k_gs):
    kmax = functools.reduce(jnp.maximum, [jnp.max(jnp.abs(g)) for g in k_gs])
    bound = HEAD_DIM * SCALE * jnp.max(jnp.abs(q_g)) * kmax
    return (bound < STATIC_SCORE_BOUND).astype(jnp.int32).reshape(1, 1)


def _merge_kernel(x_ref, yr_ref, o_ref, mg_ref, wa_ref, wo_ref, g2_ref, x1_ref, hn_ref):
    y_attn = _dot(o_ref[...].astype(BF16), wa_ref[...])
    mg = mg_ref[...]
    m = _sigmoid(mg[:, :D_MODEL]) * yr_ref[...] + _sigmoid(mg[:, D_MODEL:]) * y_attn
    x1 = x_ref[...] + _dot(m.astype(BF16), wo_ref[...])
    x1_ref[...] = x1
    hn_ref[...] = _rms(x1, g2_ref[...]).astype(BF16)


def _merge(x2, y_rnn, o_attn, mg, w_attn_out, w_out, norm2_g):
    t = x2.shape[0]
    tm = _tile(t, 512)
    rowblk = lambda n: pl.BlockSpec((tm, n), lambda i: (i, 0))
    return pl.pallas_call(
        _merge_kernel,
        grid=(t // tm,),
        in_specs=[rowblk(D_MODEL), rowblk(D_MODEL), rowblk(D_MODEL), rowblk(2 * D_MODEL),
                  _full((D_MODEL, D_MODEL)), _full((D_MODEL, D_MODEL)), _full((1, D_MODEL))],
        out_specs=[rowblk(D_MODEL), rowblk(D_MODEL)],
        out_shape=[jax.ShapeDtypeStruct((t, D_MODEL), F32), jax.ShapeDtypeStruct((t, D_MODEL), BF16)],
        compiler_params=_cparams("parallel"),
        name="merge",
    )(x2, y_rnn, o_attn, mg, w_attn_out.astype(BF16), w_out.astype(BF16), norm2_g.reshape(1, D_MODEL))


def _ffn_up_kernel(hp_ref, h_ref, wg_ref, wv_ref, cwg_ref, cwv_ref, cbg_ref, cbv_ref, act_ref, up_ref,
                   *, tm, tiles_per_seq):
    first = (pl.program_id(0) % tiles_per_seq) == 0
    keep = jnp.where(first, 0.0, 1.0)

    def conv(w_ref, cw_ref, cb_ref):
        up_ref[0:HALO, :] = _dot(hp_ref[...], w_ref[...]) * keep
        up_ref[HALO:HALO + tm, :] = _dot(h_ref[...], w_ref[...])
        cw = cw_ref[...]
        y = cb_ref[...] + cw[FFN_CONV_W - 1:FFN_CONV_W, :] * up_ref[HALO:HALO + tm, :]
        for k in range(FFN_CONV_W - 1):
            off = HALO - (FFN_CONV_W - 1) + k
            y = y + cw[k:k + 1, :] * up_ref[off:off + tm, :]
        return y

    gate = conv(wg_ref, cwg_ref, cbg_ref)
    val = conv(wv_ref, cwv_ref, cbv_ref)
    act_ref[...] = (_gelu(gate) * val).astype(BF16)


def _ffn_up(hn, w_up, conv_w, conv_b, s):
    t = hn.shape[0]
    tm = _tile(s, 1024)
    tn = 512
    nj = D_FF // tn
    wb = w_up.astype(BF16)
    cb = conv_b.reshape(1, 2 * D_FF)
    return pl.pallas_call(
        functools.partial(_ffn_up_kernel, tm=tm, tiles_per_seq=s // tm),
        grid=(t // tm, nj),
        in_specs=[pl.BlockSpec((HALO, D_MODEL), lambda i, j: (jnp.maximum(i * (tm // HALO) - 1, 0), 0)),
                  pl.BlockSpec((tm, D_MODEL), lambda i, j: (i, 0)),
                  pl.BlockSpec((D_MODEL, tn), lambda i, j: (0, j)),
                  pl.BlockSpec((D_MODEL, tn), lambda i, j: (0, j + nj)),
                  pl.BlockSpec((FFN_CONV_W, tn), lambda i, j: (0, j)),
                  pl.BlockSpec((FFN_CONV_W, tn), lambda i, j: (0, j + nj)),
                  pl.BlockSpec((1, tn), lambda i, j: (0, j)),
                  pl.BlockSpec((1, tn), lambda i, j: (0, j + nj))],
        out_specs=pl.BlockSpec((tm, tn), lambda i, j: (i, j)),
        out_shape=jax.ShapeDtypeStruct((t, D_FF), BF16),
        scratch_shapes=[pltpu.VMEM((tm + HALO, tn), F32)],
        compiler_params=_cparams("parallel", "arbitrary"),
        name="ffn_up",
    )(hn, hn, wb, wb, conv_w, conv_w, cb, cb)


def _ffn_down_kernel(x1_ref, act_ref, w_ref, o_ref):
    o_ref[...] = x1_ref[...] + _dot(act_ref[...], w_ref[...])


def _ffn_down(x1, act, w_down):
    t = x1.shape[0]
    tm = _tile(t, 512)
    return pl.pallas_call(
        _ffn_down_kernel,
        grid=(t // tm,),
        in_specs=[pl.BlockSpec((tm, D_MODEL), lambda i: (i, 0)), pl.BlockSpec((tm, D_FF), lambda i: (i, 0)),
                  _full((D_FF, D_MODEL))],
        out_specs=pl.BlockSpec((tm, D_MODEL), lambda i: (i, 0)),
        out_shape=jax.ShapeDtypeStruct((t, D_MODEL), F32),
        compiler_params=_cparams("parallel"),
        name="ffn_down",
    )(x1, act, w_down.astype(BF16))


def _cmp_rows(a, b, s):
    a = a.reshape(b, s // CMP_STRIDE, CMP_STRIDE, N_KV, HEAD_DIM).transpose(0, 3, 1, 2, 4)
    return a.reshape(b, N_KV, s // CMP_STRIDE, CMP_STRIDE * HEAD_DIM)


def kernel(x, norm1_g, w_in, rnn_conv_w, rnn_conv_b, rg_w_a, rg_b_a, rg_w_x, rg_b_x, lru_lambda, w_rnn_out, q_norm_g, k_cmp_norm_g, k_slc_norm_g, k_win_norm_g, cmp_pe_k, cmp_w1_k, cmp_w2_k, cmp_pe_v, cmp_w1_v, cmp_w2_v, w_attn_out, w_out, norm2_g, w_up, ffn_conv_w, ffn_conv_b, w_down):
    b, s, d = x.shape
    assert d == D_MODEL and s % SEL_TRIP == 0 and s // SEL_LEN <= SEL_LANES
    t = b * s
    x2 = x.reshape(t, d)

    rx, rgate, kvc, gates, mg, qpad, kaug_t, vslc, kwin_t, vwin = _in_proj(
        x2, b, s, norm1_g, w_in, q_norm_g, k_slc_norm_g, k_win_norm_g)

    y_rnn = _rglru(rx.reshape(b, s, D_RNN), rgate.reshape(b, s, D_RNN), rnn_conv_w, rnn_conv_b,
                   rg_w_a, rg_b_a, rg_w_x, rg_b_x, lru_lambda, w_rnn_out).reshape(t, d)

    kcmp, vcmp = _compress(_cmp_rows(kvc[:, :KV_W], b, s), _cmp_rows(kvc[:, KV_W:], b, s),
                           cmp_pe_k, cmp_pe_v, cmp_w1_k, cmp_w1_v, cmp_w2_k, cmp_w2_v, k_cmp_norm_g)
    kcmp_t = jnp.pad(kcmp.transpose(0, 1, 3, 2).astype(BF16), ((0, 0), (0, 0), (0, QK_LANES - HEAD_DIM), (0, 0)))
    vcmp2 = jnp.concatenate([vcmp, vcmp], axis=-1).astype(BF16)
    flag = _static_flag(q_norm_g, k_cmp_norm_g, k_slc_norm_g, k_win_norm_g)
    o_attn = _attention(flag, qpad, gates.reshape(b, s, N_KV * GATE_LANES), kcmp_t, vcmp2,
                        _overlap_matrix_t(s), kaug_t, vslc, kwin_t, vwin).reshape(t, d)

    x1, hn = _merge(x2, y_rnn, o_attn, mg, w_attn_out, w_out, norm2_g)
    act = _ffn_up(hn, w_up, ffn_conv_w, ffn_conv_b, s)
    return _ffn_down(x1, act, w_down).reshape(b, s, d)
```

```python
import functools

import numpy as np
import jax
import jax.numpy as jnp
from jax import lax
from jax.experimental import pallas as pl
from jax.experimental.pallas import tpu as pltpu

F32 = jnp.float32
BF16 = jnp.bfloat16

D_MODEL = 1024
D_RNN = 1344
RNN_BLOCKS = 16
RNN_CONV_W = 4
LRU_C = 8.0
N_HEADS = 16
N_KV = 4
HEAD_DIM = 64
GROUP = N_HEADS // N_KV
KV_W = N_KV * HEAD_DIM
CMP_LEN = 32
CMP_STRIDE = 16
CMP_HIDDEN = 128
SEL_LEN = 64
N_SELECT = 16
N_FORCED = 3
WINDOW = 512
D_FF = 3 * D_MODEL
FFN_CONV_W = 3
EPS = 1e-6
NEG = -1e30
FORCE = 1e6
SCALE = HEAD_DIM ** -0.5

SEL_LANES = 128
V_LANES = 128
QK_LANES = 128
GATE_LANES = 128
TQ = 256
SEL_TRIP = 1024
LOG2E = float(np.log2(np.e))
HALO = 8
VMEM_LIMIT = 56 * 1024 * 1024
STATIC_SCORE_BOUND = 40.0


def _cparams(*sem):
    return pltpu.CompilerParams(dimension_semantics=sem, vmem_limit_bytes=VMEM_LIMIT)


def _gelu(x):
    c = np.float32(np.sqrt(2.0 / np.pi))
    return x * (0.5 * (1.0 + jnp.tanh(c * (x + 0.044715 * (x * x * x)))))


def _sigmoid(x):
    return 0.5 + 0.5 * jnp.tanh(0.5 * x)


def _rms(x, g):
    ms = jnp.mean(x * x, axis=-1, keepdims=True)
    return x * lax.rsqrt(ms + EPS) * g


def _dot(a, b):
    return jnp.dot(a, b, preferred_element_type=F32)


def _dot_nt(a, b):
    return lax.dot_general(a, b, (((1,), (1,)), ((), ())), preferred_element_type=F32)


def _split_bf16(x):
    hi = x.astype(BF16)
    lo = (x - hi.astype(F32)).astype(BF16)
    return hi, lo


def _tile(n, want):
    t = min(n, want)
    while n % t:
        t //= 2
    return t


def _full(shape):
    nd = len(shape)
    return pl.BlockSpec(shape, lambda *_: (0,) * nd)


def _in_proj_kernel(x_ref, g_ref, w_rx, w_rg, w_q, w_kv, w_g, w_mg, qg_ref, gs_ref, gw_ref, eye_ref,
                    kw0_ref, vw0_ref,
                    o_rx, o_rg, o_cmp, o_g, o_mg, q_ref, ka_ref, vs_ref, kw_ref, vw_ref, *, tiles_per_seq):
    del kw0_ref, vw0_ref
    h = _rms(x_ref[...], g_ref[...]).astype(BF16)

    q = _dot(h, w_q[...])
    kv = _dot(h, w_kv[...])
    o_rx[...] = _dot(h, w_rx[...])
    for g in range(N_KV):
        for r in range(GROUP):
            lo = (g * GROUP + r) * HEAD_DIM
            y = _rms(q[:, lo:lo + HEAD_DIM], qg_ref[...]) * (SCALE * LOG2E)
            q_ref[0, g, 0, :, r * TQ:(r + 1) * TQ] = _dot_nt(eye_ref[...], y.astype(BF16)).astype(BF16)

    o_rg[...] = _dot(h, w_rg[...])
    o_cmp[...] = kv[:, :2 * KV_W]
    pos = (pl.program_id(0) % tiles_per_seq) * TQ + lax.broadcasted_iota(jnp.int32, (TQ, 1), 0)
    blk = lax.shift_right_logical(pos, SEL_LEN.bit_length() - 1)
    onehot = (lax.broadcasted_iota(jnp.int32, (1, SEL_LANES), 1) == blk).astype(BF16)
    zeros = jnp.zeros((TQ, QK_LANES - HEAD_DIM), BF16)
    ones = jnp.ones((V_LANES - HEAD_DIM, TQ), BF16)
    for g in range(N_KV):
        lo = g * HEAD_DIM
        ka_ref[0, g, :, 0:SEL_LANES] = onehot
        ka_ref[0, g, :, SEL_LANES:SEL_LANES + HEAD_DIM] = _rms(
            kv[:, 2 * KV_W + lo:2 * KV_W + lo + HEAD_DIM], gs_ref[...]).astype(BF16)
        ka_ref[0, g, :, SEL_LANES + HEAD_DIM:SEL_LANES + QK_LANES] = zeros
        kw_ref[0, g, :, 0:HEAD_DIM] = _rms(kv[:, 4 * KV_W + lo:4 * KV_W + lo + HEAD_DIM], gw_ref[...]).astype(BF16)
        kw_ref[0, g, :, HEAD_DIM:QK_LANES] = zeros
        for src, dst in ((3, vs_ref), (5, vw_ref)):
            v = kv[:, src * KV_W + lo:src * KV_W + lo + HEAD_DIM].astype(BF16)
            dst[0, g, 0:HEAD_DIM, :] = _dot_nt(eye_ref[0:HEAD_DIM, :], v).astype(BF16)
            dst[0, g, HEAD_DIM:V_LANES, :] = ones

    o_g[...] = _dot(h, w_g[...])
    o_mg[...] = _dot(h, w_mg[...])


def _in_proj(x2, b, s, norm1_g, w_in, q_g, g_slc, g_win):
    t = x2.shape[0]
    nq = s // TQ
    pad_blocks = WINDOW // TQ
    kvw = 6 * KV_W
    o0 = 2 * D_RNN
    o1 = o0 + N_HEADS * HEAD_DIM
    o2 = o1 + kvw
    o3 = o2 + 3 * N_HEADS
    wb = w_in.astype(BF16)
    w_g = jnp.pad(wb[:, o2:o3].reshape(D_MODEL, N_KV, 3 * GROUP), ((0, 0), (0, 0), (0, GATE_LANES - 3 * GROUP)))
    w_g = w_g.reshape(D_MODEL, N_KV * GATE_LANES)
    ws = [wb[:, :D_RNN], wb[:, D_RNN:o0], wb[:, o0:o1], wb[:, o1:o2], w_g, wb[:, o3:]]
    eye = jnp.asarray(np.eye(QK_LANES, HEAD_DIM), BF16)
    gain = lambda v: v.reshape(1, HEAD_DIM)
    rows = lambda n: pl.BlockSpec((TQ, n), lambda i: (i, 0))
    flat = lambda n: jax.ShapeDtypeStruct((t, n), F32)
    kwin0 = jnp.zeros((b, N_KV, s + WINDOW, QK_LANES), BF16)
    vwin0 = jnp.zeros((b, N_KV, V_LANES, s + WINDOW), BF16)
    n_in = 14
    return pl.pallas_call(
        functools.partial(_in_proj_kernel, tiles_per_seq=nq),
        grid=(t // TQ,),
        in_specs=[rows(D_MODEL), _full((1, D_MODEL))] + [_full(w.shape) for w in ws]
        + [_full((1, HEAD_DIM))] * 3 + [_full((QK_LANES, HEAD_DIM))]
        + [pl.BlockSpec(memory_space=pl.ANY)] * 2,
        out_specs=[rows(D_RNN), rows(D_RNN), rows(2 * KV_W), rows(N_KV * GATE_LANES), rows(2 * D_MODEL),
                   pl.BlockSpec((1, N_KV, 1, QK_LANES, GROUP * TQ), lambda i: (i // nq, 0, i % nq, 0, 0)),
                   pl.BlockSpec((1, N_KV, TQ, SEL_LANES + QK_LANES), lambda i: (i // nq, 0, i % nq, 0)),
                   pl.BlockSpec((1, N_KV, V_LANES, TQ), lambda i: (i // nq, 0, 0, i % nq)),
                   pl.BlockSpec((1, N_KV, TQ, QK_LANES), lambda i: (i // nq, 0, i % nq + pad_blocks, 0)),
                   pl.BlockSpec((1, N_KV, V_LANES, TQ), lambda i: (i // nq, 0, 0, i % nq + pad_blocks))],
        out_shape=[flat(D_RNN), flat(D_RNN), flat(2 * KV_W), flat(N_KV * GATE_LANES), flat(2 * D_MODEL),
                   jax.ShapeDtypeStruct((b, N_KV, nq, QK_LANES, GROUP * TQ), BF16),
                   jax.ShapeDtypeStruct((b, N_KV, s, SEL_LANES + QK_LANES), BF16),
                   jax.ShapeDtypeStruct((b, N_KV, V_LANES, s), BF16),
                   jax.ShapeDtypeStruct(kwin0.shape, BF16), jax.ShapeDtypeStruct(vwin0.shape, BF16)],
        input_output_aliases={n_in - 2: 8, n_in - 1: 9},
        compiler_params=_cparams("parallel"),
        name="in_proj",
    )(x2, norm1_g.reshape(1, D_MODEL), *ws, gain(q_g), gain(g_slc), gain(g_win), eye, kwin0, vwin0)


def _rglru_kernel(rx_ref, rg_ref, cw_ref, cb_ref, wa_ref, ba_ref, wx_ref, bx_ref, lam_ref, wo_ref,
                  y_ref, xs_ref, a_ref, b_ref, carry_ref, *, tt):
    ti = pl.program_id(1)

    @pl.when(ti == 0)
    def _():
        xs_ref[0:HALO, :] = jnp.zeros((HALO, D_RNN), F32)
        carry_ref[...] = jnp.zeros((HALO, D_RNN), F32)

    xs_ref[HALO:HALO + tt, :] = rx_ref[0]
    cw = cw_ref[...]
    x = cb_ref[...] + cw[RNN_CONV_W - 1:RNN_CONV_W, :] * xs_ref[HALO:HALO + tt, :]
    for k in range(RNN_CONV_W - 1):
        off = HALO - (RNN_CONV_W - 1) + k
        x = x + cw[k:k + 1, :] * xs_ref[off:off + tt, :]
    xs_ref[0:HALO, :] = xs_ref[tt:tt + HALO, :]

    xb = x.astype(BF16)
    r = _sigmoid(_dot(xb, wa_ref[...]) + ba_ref[...])
    gi = _sigmoid(_dot(xb, wx_ref[...]) + bx_ref[...])
    lam = lam_ref[...]
    log_sig = jnp.minimum(lam, 0.0) - jnp.log1p(jnp.exp(-jnp.abs(lam)))
    log_a = LRU_C * r * log_sig
    a = jnp.exp(log_a)
    y = 1.0 - a * a
    u = jnp.where(y > 0.0, y * lax.rsqrt(y), 0.0) * (gi * x)

    sub = lax.broadcasted_iota(jnp.int32, (tt, 1), 0) & (HALO - 1)
    for d in (1, 2, 4):
        keep = sub >= d
        a_sh = pltpu.roll(a, d, 0)
        u_sh = pltpu.roll(u, d, 0)
        u = jnp.where(keep, a * u_sh + u, u)
        a = jnp.where(keep, a * a_sh, a)
    a_ref[...] = a
    b_ref[...] = u

    def group(gidx, carry):
        rows = pl.ds(pl.multiple_of(gidx * HALO, HALO), HALO)
        h = a_ref[rows, :] * carry + b_ref[rows, :]
        b_ref[rows, :] = h
        return jnp.broadcast_to(h[HALO - 1:HALO, :], (HALO, D_RNN))

    carry_ref[...] = lax.fori_loop(0, tt // HALO, group, carry_ref[...], unroll=4)

    hg = (b_ref[...] * _gelu(rg_ref[0])).astype(BF16)
    y_ref[0] = _dot(hg, wo_ref[...])


def _block_diag(w):
    n, c, d = w.shape
    return jnp.einsum("ncd,nm->ncmd", w, jnp.eye(n, dtype=w.dtype)).reshape(n * c, n * d)


def _rglru(rx3, rg3, conv_w, conv_b, w_a, b_a, w_x, b_x, lam, w_out):
    b, s, _ = rx3.shape
    tt = _tile(s, 256)
    row = lambda v: v.reshape(1, D_RNN)
    return pl.pallas_call(
        functools.partial(_rglru_kernel, tt=tt),
        grid=(b, s // tt),
        in_specs=[pl.BlockSpec((1, tt, D_RNN), lambda i, j: (i, j, 0)),
                  pl.BlockSpec((1, tt, D_RNN), lambda i, j: (i, j, 0)),
                  _full((RNN_CONV_W, D_RNN)), _full((1, D_RNN)),
                  _full((D_RNN, D_RNN)), _full((1, D_RNN)),
                  _full((D_RNN, D_RNN)), _full((1, D_RNN)),
                  _full((1, D_RNN)), _full((D_RNN, D_MODEL))],
        out_specs=pl.BlockSpec((1, tt, D_MODEL), lambda i, j: (i, j, 0)),
        out_shape=jax.ShapeDtypeStruct((b, s, D_MODEL), F32),
        scratch_shapes=[pltpu.VMEM((tt + HALO, D_RNN), F32), pltpu.VMEM((tt, D_RNN), F32),
                        pltpu.VMEM((tt, D_RNN), F32), pltpu.VMEM((HALO, D_RNN), F32)],
        compiler_params=_cparams("arbitrary", "arbitrary"),
        name="rglru",
    )(rx3, rg3, conv_w, row(conv_b), _block_diag(w_a).astype(BF16), row(b_a),
      _block_diag(w_x).astype(BF16), row(b_x), row(lam), w_out.astype(BF16))


def _compress_kernel(rk_ref, rv_ref, pek_ref, pev_ref, w1k_ref, w1v_ref, w2k_ref, w2v_ref, gk_ref,
                     kc_ref, vc_ref):
    half = CMP_STRIDE * HEAD_DIM
    ncp = rk_ref.shape[2]

    def mlp(r_ref, pe_ref, w1_ref, w2_ref):
        r = r_ref[0, 0]
        pe = pe_ref[...]
        lo = _dot((r + pe[:, :half]).astype(BF16), w1_ref[:half, :])
        hi = _dot((r + pe[:, half:]).astype(BF16), w1_ref[half:, :])
        hid = lo + pltpu.roll(hi, ncp - 1, 0)
        return _dot(_gelu(hid).astype(BF16), w2_ref[...])

    kc_ref[0, 0] = _rms(mlp(rk_ref, pek_ref, w1k_ref, w2k_ref), gk_ref[...])
    vc_ref[0, 0] = mlp(rv_ref, pev_ref, w1v_ref, w2v_ref)


def _compress(rk, rv, pe_k, pe_v, w1k, w1v, w2k, w2v, g_cmp):
    b, g, ncp, wide = rk.shape
    blk = pl.BlockSpec((1, 1, ncp, wide), lambda i, j: (i, j, 0, 0))
    out = pl.BlockSpec((1, 1, ncp, HEAD_DIM), lambda i, j: (i, j, 0, 0))
    w1 = _full((CMP_LEN * HEAD_DIM, CMP_HIDDEN))
    w2 = _full((CMP_HIDDEN, HEAD_DIM))
    pe = _full((1, CMP_LEN * HEAD_DIM))
    return pl.pallas_call(
        _compress_kernel,
        grid=(b, g),
        in_specs=[blk, blk, pe, pe, w1, w1, w2, w2, _full((1, HEAD_DIM))],
        out_specs=[out, out],
        out_shape=[jax.ShapeDtypeStruct((b, g, ncp, HEAD_DIM), F32)] * 2,
        compiler_params=_cparams("parallel", "parallel"),
        name="compress",
    )(rk, rv, pe_k.reshape(1, -1), pe_v.reshape(1, -1), w1k.astype(BF16), w1v.astype(BF16),
      w2k.astype(BF16), w2v.astype(BF16), g_cmp.reshape(1, HEAD_DIM))


def _exp_terms(sm, static):
    if static:
        return jnp.exp2(sm)
    return jnp.exp2(sm - jnp.max(sm, axis=0, keepdims=True))


def _normalise(acc):
    return acc[:HEAD_DIM] * (1.0 / acc[HEAD_DIM:])


def _attn_kernel(flag_ref, q_ref, gt_ref, kct_ref, vc_ref, ovt_ref, kat_ref, vs_ref, kwt_ref, vw_ref,
                 db_ref, wb_ref, ge_ref, o_ref):
    qi = pl.program_id(2)
    st = qi * TQ
    rows = GROUP * TQ
    ncp = kct_ref.shape[3]
    static_ok = flag_ref[0, 0] != 0
    both = lambda f: lax.cond(static_ok, lambda: f(True), lambda: f(False))

    qsb = q_ref[0, 0, 0]
    t_row = st + (lax.broadcasted_iota(jnp.int32, (rows, 1), 0) & (TQ - 1))

    def compressed(static):
        def over(n):
            cidx = lax.broadcasted_iota(jnp.int32, (1, n), 1)
            valid = (cidx * CMP_STRIDE + (CMP_LEN - 1)) <= t_row
            sm = jnp.where(valid, _dot(qsb, kct_ref[0, 0, :, 0:n]), NEG)
            e = _exp_terms(sm, True) if static else jnp.where(valid, _exp_terms(sm, False), 0.0)
            l = jnp.sum(e, axis=-1, keepdims=True)
            p = e * (1.0 / jnp.where(l > 0.0, l, 1.0))
            psum = p[0:TQ] + p[TQ:2 * TQ] + p[2 * TQ:3 * TQ] + p[3 * TQ:4 * TQ]
            if n < ncp:
                psum = jnp.concatenate([psum, jnp.zeros((TQ, ncp - n), F32)], axis=1)
            return _dot(p.astype(BF16), vc_ref[0, 0, 0:n, :]), psum

        half = ncp // 2
        if half % 128:
            return over(ncp)
        return lax.cond(st + TQ <= half * CMP_STRIDE + CMP_LEN - 1, lambda: over(half), lambda: over(ncp))

    o_cmp, psum = both(compressed)

    p_hi, p_lo = _split_bf16(psum)
    imp_t = _dot_nt(ovt_ref[...], p_hi) + _dot_nt(ovt_ref[...], p_lo)
    jb = lax.broadcasted_iota(jnp.int32, (SEL_LANES, 1), 0)
    jbf = jb.astype(F32)
    cur = lax.shift_right_logical(st + lax.broadcasted_iota(jnp.int32, (1, TQ), 1), 6)
    forced = (jb == 0) | (jb == cur) | (jb == cur - 1)
    sel = forced
    work = jnp.where((jb <= cur) & jnp.logical_not(forced), imp_t, -FORCE)
    for _ in range(N_SELECT - N_FORCED):
        mx = jnp.max(work, axis=0, keepdims=True)
        first = jnp.min(jnp.where(work == mx, jbf, float(SEL_LANES)), axis=0, keepdims=True)
        pick = jbf == first
        sel = sel | pick
        work = jnp.where(pick, -jnp.inf, work)
    first_blk = lax.shift_right_logical(st, 6)

    def with_bias(keep):
        bias = jnp.where(sel & keep, 0.0, NEG).T.astype(BF16)
        return jnp.concatenate([jnp.concatenate([bias] * GROUP, axis=0), qsb], axis=1)

    qaug = with_bias(jb < first_blk)
    qaug_d = with_bias((jb >= first_blk) & (jb <= cur))

    dkeys = pl.ds(pl.multiple_of(st, TQ), TQ)
    wkeys = pl.ds(pl.multiple_of(st, TQ), WINDOW + TQ)
    n_trips = lax.shift_right_logical(st + SEL_TRIP - 1, SEL_TRIP.bit_length() - 1)

    def branches(static):
        sd = _per_head(_dot(qaug_d, kat_ref[0, 0, :, dkeys]), db_ref[...])
        vd = vs_ref[0, 0, dkeys, :]
        if static:
            acc0 = _dot(_exp_terms(sd, True).astype(BF16), vd)

            def trip(c, acc):
                keys = pl.ds(pl.multiple_of(c * SEL_TRIP, SEL_TRIP), SEL_TRIP)
                pc = jnp.exp2(_dot(qaug, kat_ref[0, 0, :, keys]))
                return acc + _dot(pc.astype(BF16), vs_ref[0, 0, keys, :])

            acc_s = lax.fori_loop(0, n_trips, trip, acc0)
        else:
            m0 = jnp.max(sd, axis=-1, keepdims=True)
            acc0 = _dot(jnp.exp2(sd - m0).astype(BF16), vd)

            def trip(c, carry):
                m, acc = carry
                keys = pl.ds(pl.multiple_of(c * SEL_TRIP, SEL_TRIP), SEL_TRIP)
                sc = _dot(qaug, kat_ref[0, 0, :, keys])
                mn = jnp.maximum(m, jnp.max(sc, axis=-1, keepdims=True))
                acc = jnp.exp2(m - mn) * acc + _dot(jnp.exp2(sc - mn).astype(BF16), vs_ref[0, 0, keys, :])
                return mn, acc

            _, acc_s = lax.fori_loop(0, n_trips, trip, (m0, acc0))
        sw = _per_head(_dot(qsb, kwt_ref[0, 0, :, wkeys]), wb_ref[0])
        acc_w = _dot(_exp_terms(sw, static).astype(BF16), vw_ref[0, 0, wkeys, :])
        return acc_s, acc_w

    acc_s, acc_w = both(branches)
    g_hi, g_lo = _split_bf16(_sigmoid(gt_ref[0]))
    out = jnp.zeros((TQ, KV_W), F32)
    for j, y in enumerate((o_cmp, _normalise(acc_s), _normalise(acc_w))):
        out = out + (_dot(g_hi, ge_ref[j]) + _dot(g_lo, ge_ref[j])) * _assemble(y)
    o_ref[0] = out


def _attention(flag, qpad, gates3, kcmp_t, vcmp, overlap_t, kaug_t, vslc, kwin_t, vwin):
    b, _, n_tiles, rows, _ = qpad.shape
    s = n_tiles * TQ
    ncp = kcmp_t.shape[3]
    sp = kwin_t.shape[3]
    per_bg = lambda shape: pl.BlockSpec((1, 1) + shape, lambda i, j, k: (i, j, 0, 0))
    return pl.pallas_call(
        _attn_kernel,
        grid=(b, N_KV, n_tiles),
        in_specs=[pl.BlockSpec(memory_space=pltpu.SMEM),
                  pl.BlockSpec((1, 1, 1, rows, QK_LANES), lambda i, j, k: (i, j, k, 0, 0)),
                  pl.BlockSpec((1, TQ, GATE_LANES), lambda i, j, k: (i, k, j)),
                  per_bg((QK_LANES, ncp)), per_bg((ncp, 2 * HEAD_DIM)), _full((SEL_LANES, ncp)),
                  per_bg((SEL_LANES + QK_LANES, s)), per_bg((s, V_LANES)),
                  per_bg((QK_LANES, sp)), per_bg((sp, V_LANES)),
                  _full((TQ, TQ)),
                  pl.BlockSpec((1, TQ, WINDOW + TQ), lambda i, j, k: (jnp.minimum(k, WINDOW // TQ), 0, 0)),
                  _full((3, GATE_LANES, KV_W))],
        out_specs=pl.BlockSpec((1, TQ, KV_W), lambda i, j, k: (i, k, j)),
        out_shape=jax.ShapeDtypeStruct((b, s, N_HEADS * HEAD_DIM), F32),
        compiler_params=_cparams("parallel", "parallel", "arbitrary"),
        name="nsa_attention",
    )(flag, qpad, gates3, kcmp_t, vcmp, overlap_t, kaug_t, vslc, kwin_t, vwin, *_band_biases(),
      _gate_expand())


def _gate_expand():
    e = np.zeros((3, GATE_LANES, KV_W), np.float32)
    for r in range(GROUP):
        for j in range(3):
            e[j, 3 * r + j, r * HEAD_DIM:(r + 1) * HEAD_DIM] = 1.0
    return jnp.asarray(e, dtype=BF16)


def _band_biases():
    q = np.arange(TQ)[:, None]
    causal = np.where(np.arange(TQ)[None, :] <= q, 0.0, NEG)
    kw = np.arange(WINDOW + TQ)[None, :] - WINDOW
    in_band = (kw <= q) & (kw > q - WINDOW)
    bands = [np.where(in_band & (kw + tile * TQ >= 0), 0.0, NEG) for tile in range(WINDOW // TQ + 1)]
    return jnp.asarray(causal, F32), jnp.asarray(np.stack(bands), F32)


def _attn_t_kernel(flag_ref, q_ref, gt_ref, kc_ref, vct_ref, ovt_ref, ka_ref, vst_ref, kw_ref, vwt_ref, wb_ref,
                   o_ref):
    qi = pl.program_id(2)
    st = qi * TQ
    cols = GROUP * TQ
    ncp = kc_ref.shape[2]
    static_ok = flag_ref[0, 0] != 0
    both = lambda f: lax.cond(static_ok, lambda: f(True), lambda: f(False))

    qt = q_ref[0, 0, 0]
    q_loc = lax.broadcasted_iota(jnp.int32, (1, cols), 1) & (TQ - 1)
    t_col = st + q_loc

    def compressed(static):
        def over(n):
            cidx = lax.broadcasted_iota(jnp.int32, (n, 1), 0)
            valid = (cidx * CMP_STRIDE + (CMP_LEN - 1)) <= t_col
            sm = jnp.where(valid, _dot(kc_ref[0, 0, 0:n, :], qt), NEG)
            e = _exp_terms(sm, True) if static else jnp.where(valid, _exp_terms(sm, False), 0.0)
            l = jnp.sum(e, axis=0, keepdims=True)
            p = e * (1.0 / jnp.where(l > 0.0, l, 1.0))
            psum = p[:, 0:TQ] + p[:, TQ:2 * TQ] + p[:, 2 * TQ:3 * TQ] + p[:, 3 * TQ:4 * TQ]
            if n < ncp:
                psum = jnp.concatenate([psum, jnp.zeros((ncp - n, TQ), F32)], axis=0)
            return _dot(vct_ref[0, 0, :, 0:n], p.astype(BF16)), psum

        half = ncp // 2
        if half % 128:
            return over(ncp)
        return lax.cond(st + TQ <= half * CMP_STRIDE + CMP_LEN - 1, lambda: over(half), lambda: over(ncp))

    o_cmp, psum = both(compressed)

    p_hi, p_lo = _split_bf16(psum)
    imp_t = _dot(ovt_ref[...], p_hi) + _dot(ovt_ref[...], p_lo)
    jb = lax.broadcasted_iota(jnp.int32, (SEL_LANES, 1), 0)
    jbf = jb.astype(F32)
    cur = lax.shift_right_logical(st + lax.broadcasted_iota(jnp.int32, (1, TQ), 1), SEL_LEN.bit_length() - 1)
    forced = (jb == 0) | (jb == cur) | (jb == cur - 1)
    sel = forced
    work = jnp.where((jb <= cur) & jnp.logical_not(forced), imp_t, -FORCE)
    for _ in range(N_SELECT - N_FORCED):
        mx = jnp.max(work, axis=0, keepdims=True)
        first = jnp.min(jnp.where(work == mx, jbf, float(SEL_LANES)), axis=0, keepdims=True)
        pick = jbf == first
        sel = sel | pick
        work = jnp.where(pick, -jnp.inf, work)
    bias = jnp.where(sel & (jb <= cur), 0.0, NEG).astype(BF16)
    qaug = jnp.concatenate([jnp.concatenate([bias] * GROUP, axis=1), qt], axis=0)

    n_full = lax.shift_right_logical(st, SEL_TRIP.bit_length() - 1)
    causal = jnp.where(lax.broadcasted_iota(jnp.int32, (TQ, 1), 0) <= q_loc, 0.0, NEG)

    def selected(static):
        def scores(keys):
            return _dot(ka_ref[0, 0, keys, :], qaug), vst_ref[0, 0, :, keys]

        def last(before):
            s_t, v_t = scores(pl.ds(pl.multiple_of(st - before, TQ), before + TQ))
            own = s_t[before:] + causal
            s_t = jnp.concatenate([s_t[:before], own], axis=0) if before else own
            if static:
                return _dot(v_t, _exp_terms(s_t, True).astype(BF16))
            m = jnp.max(s_t, axis=0, keepdims=True)
            return m, _dot(v_t, jnp.exp2(s_t - m).astype(BF16))

        rem = lax.shift_right_logical(st & (SEL_TRIP - 1), TQ.bit_length() - 1)
        first = lax.switch(rem, [functools.partial(last, i * TQ) for i in range(SEL_TRIP // TQ)])
        trip_keys = lambda c: pl.ds(pl.multiple_of(c * SEL_TRIP, SEL_TRIP), SEL_TRIP)
        if static:
            def trip(c, acc):
                s_t, v_t = scores(trip_keys(c))
                return acc + _dot(v_t, jnp.exp2(s_t).astype(BF16))

            return lax.fori_loop(0, n_full, trip, first)

        def trip(c, carry):
            m, acc = carry
            s_t, v_t = scores(trip_keys(c))
            mn = jnp.maximum(m, jnp.max(s_t, axis=0, keepdims=True))
            return mn, jnp.exp2(m - mn) * acc + _dot(v_t, jnp.exp2(s_t - mn).astype(BF16))

        return lax.fori_loop(0, n_full, trip, first)[1]

    def window(static):
        wkeys = pl.ds(pl.multiple_of(st, TQ), WINDOW + TQ)
        s_t = _dot(kw_ref[0, 0, wkeys, :], qt)
        e = jnp.concatenate([_exp_terms(s_t[:, r * TQ:(r + 1) * TQ] + wb_ref[0], static)
                             for r in range(GROUP)], axis=1)
        return _dot(vwt_ref[0, 0, :, wkeys], e.astype(BF16))

    o_slc = _normalise(both(selected))
    o_win = _normalise(both(window))

    sg = _sigmoid(gt_ref[0]).T
    heads = []
    for r in range(GROUP):
        acc = jnp.zeros((HEAD_DIM, TQ), F32)
        for j, y in enumerate((o_cmp, o_slc, o_win)):
            acc = acc + sg[3 * r + j:3 * r + j + 1, :] * y[:, r * TQ:(r + 1) * TQ]
        heads.append(acc)
    o_ref[0] = jnp.concatenate(heads, axis=0).T


def _attention_t(flag, q_t, gates3, kcmp, vcmp_t, overlap_t, kaug, vslc_t, kwin, vwin_t):
    b, _, n_tiles, _, cols = q_t.shape
    s = n_tiles * TQ
    ncp = kcmp.shape[2]
    sp = kwin.shape[2]
    per_bg = lambda shape: pl.BlockSpec((1, 1) + shape, lambda i, j, k: (i, j, 0, 0))
    return pl.pallas_call(
        _attn_t_kernel,
        grid=(b, N_KV, n_tiles),
        in_specs=[pl.BlockSpec(memory_space=pltpu.SMEM),
                  pl.BlockSpec((1, 1, 1, QK_LANES, cols), lambda i, j, k: (i, j, k, 0, 0)),
                  pl.BlockSpec((1, TQ, GATE_LANES), lambda i, j, k: (i, k, j)),
                  per_bg((ncp, QK_LANES)), per_bg((HEAD_DIM, ncp)), _full((SEL_LANES, ncp)),
                  per_bg((s, SEL_LANES + QK_LANES)), per_bg((V_LANES, s)),
                  per_bg((sp, QK_LANES)), per_bg((V_LANES, sp)),
                  pl.BlockSpec((1, WINDOW + TQ, TQ), lambda i, j, k: (jnp.minimum(k, WINDOW // TQ), 0, 0))],
        out_specs=pl.BlockSpec((1, TQ, KV_W), lambda i, j, k: (i, k, j)),
        out_shape=jax.ShapeDtypeStruct((b, s, N_HEADS * HEAD_DIM), F32),
        compiler_params=_cparams("parallel", "parallel", "arbitrary"),
        name="nsa_attention",
    )(flag, q_t, gates3, kcmp, vcmp_t, overlap_t, kaug, vslc_t, kwin, vwin_t, _band_biases_t())


def _band_biases_t():
    q = np.arange(TQ)[None, :]
    kw = np.arange(WINDOW + TQ)[:, None] - WINDOW
    in_band = (kw <= q) & (kw > q - WINDOW)
    bands = [np.where(in_band & (kw + tile * TQ >= 0), 0.0, NEG) for tile in range(WINDOW // TQ + 1)]
    return jnp.asarray(np.stack(bands), F32)


def _overlap_matrix_t(s):
    ncp = s // CMP_STRIDE
    i = np.arange(ncp)[None, :]
    j = np.arange(SEL_LANES)[:, None]
    lo = np.maximum(i * CMP_STRIDE, j * SEL_LEN)
    hi = np.minimum(i * CMP_STRIDE + CMP_LEN, (j + 1) * SEL_LEN)
    ov = np.maximum(hi - lo, 0) / CMP_STRIDE
    ov = ov * (i < (s - CMP_LEN) // CMP_STRIDE + 1) * (j < s // SEL_LEN)
    return jnp.asarray(ov, dtype=BF16)


def _static_flag(q_g, *k_gs):
    kmax = functools.reduce(jnp.maximum, [jnp.max(jnp.abs(g)) for g in k_gs])
    bound = HEAD_DIM * SCALE * jnp.max(jnp.abs(q_g)) * kmax
    return (bound < STATIC_SCORE_BOUND).astype(jnp.int32).reshape(1, 1)


def _merge_kernel(x_ref, yr_ref, o_ref, mg_ref, wa_ref, wo_ref, g2_ref, x1_ref, hn_ref):
    y_attn = _dot(o_ref[...].astype(BF16), wa_ref[...])
    mg = mg_ref[...]
    m = _sigmoid(mg[:, :D_MODEL]) * yr_ref[...] + _sigmoid(mg[:, D_MODEL:]) * y_attn
    x1 = x_ref[...] + _dot(m.astype(BF16), wo_ref[...])
    x1_ref[...] = x1
    hn_ref[...] = _rms(x1, g2_ref[...]).astype(BF16)


def _merge(x2, y_rnn, o_attn, mg, w_attn_out, w_out, norm2_g):
    t = x2.shape[0]
    tm = _tile(t, 512)
    rowblk = lambda n: pl.BlockSpec((tm, n), lambda i: (i, 0))
    return pl.pallas_call(
        _merge_kernel,
        grid=(t // tm,),
        in_specs=[rowblk(D_MODEL), rowblk(D_MODEL), rowblk(D_MODEL), rowblk(2 * D_MODEL),
                  _full((D_MODEL, D_MODEL)), _full((D_MODEL, D_MODEL)), _full((1, D_MODEL))],
        out_specs=[rowblk(D_MODEL), rowblk(D_MODEL)],
        out_shape=[jax.ShapeDtypeStruct((t, D_MODEL), F32), jax.ShapeDtypeStruct((t, D_MODEL), BF16)],
        compiler_params=_cparams("parallel"),
        name="merge",
    )(x2, y_rnn, o_attn, mg, w_attn_out.astype(BF16), w_out.astype(BF16), norm2_g.reshape(1, D_MODEL))


def _ffn_up_kernel(hp_ref, h_ref, wg_ref, wv_ref, cwg_ref, cwv_ref, cbg_ref, cbv_ref, act_ref, up_ref,
                   *, tm, tiles_per_seq):
    first = (pl.program_id(0) % tiles_per_seq) == 0
    keep = jnp.where(first, 0.0, 1.0)

    def conv(w_ref, cw_ref, cb_ref):
        up_ref[0:HALO, :] = _dot(hp_ref[...], w_ref[...]) * keep
        up_ref[HALO:HALO + tm, :] = _dot(h_ref[...], w_ref[...])
        cw = cw_ref[...]
        y = cb_ref[...] + cw[FFN_CONV_W - 1:FFN_CONV_W, :] * up_ref[HALO:HALO + tm, :]
        for k in range(FFN_CONV_W - 1):
            off = HALO - (FFN_CONV_W - 1) + k
            y = y + cw[k:k + 1, :] * up_ref[off:off + tm, :]
        return y

    gate = conv(wg_ref, cwg_ref, cbg_ref)
    val = conv(wv_ref, cwv_ref, cbv_ref)
    act_ref[...] = (_gelu(gate) * val).astype(BF16)


def _ffn_up(hn, w_up, conv_w, conv_b, s):
    t = hn.shape[0]
    tm = _tile(s, 1024)
    tn = 512
    nj = D_FF // tn
    wb = w_up.astype(BF16)
    cb = conv_b.reshape(1, 2 * D_FF)
    return pl.pallas_call(
        functools.partial(_ffn_up_kernel, tm=tm, tiles_per_seq=s // tm),
        grid=(t // tm, nj),
        in_specs=[pl.BlockSpec((HALO, D_MODEL), lambda i, j: (jnp.maximum(i * (tm // HALO) - 1, 0), 0)),
                  pl.BlockSpec((tm, D_MODEL), lambda i, j: (i, 0)),
                  pl.BlockSpec((D_MODEL, tn), lambda i, j: (0, j)),
                  pl.BlockSpec((D_MODEL, tn), lambda i, j: (0, j + nj)),
                  pl.BlockSpec((FFN_CONV_W, tn), lambda i, j: (0, j)),
                  pl.BlockSpec((FFN_CONV_W, tn), lambda i, j: (0, j + nj)),
                  pl.BlockSpec((1, tn), lambda i, j: (0, j)),
                  pl.BlockSpec((1, tn), lambda i, j: (0, j + nj))],
        out_specs=pl.BlockSpec((tm, tn), lambda i, j: (i, j)),
        out_shape=jax.ShapeDtypeStruct((t, D_FF), BF16),
        scratch_shapes=[pltpu.VMEM((tm + HALO, tn), F32)],
        compiler_params=_cparams("parallel", "arbitrary"),
        name="ffn_up",
    )(hn, hn, wb, wb, conv_w, conv_w, cb, cb)


def _ffn_down_kernel(x1_ref, act_ref, w_ref, o_ref):
    o_ref[...] = x1_ref[...] + _dot(act_ref[...], w_ref[...])


def _ffn_down(x1, act, w_down):
    t = x1.shape[0]
    tm = _tile(t, 512)
    return pl.pallas_call(
        _ffn_down_kernel,
        grid=(t // tm,),
        in_specs=[pl.BlockSpec((tm, D_MODEL), lambda i: (i, 0)), pl.BlockSpec((tm, D_FF), lambda i: (i, 0)),
                  _full((D_FF, D_MODEL))],
        out_specs=pl.BlockSpec((tm, D_MODEL), lambda i: (i, 0)),
        out_shape=jax.ShapeDtypeStruct((t, D_MODEL), F32),
        compiler_params=_cparams("parallel"),
        name="ffn_down",
    )(x1, act, w_down.astype(BF16))


def _cmp_rows(a, b, s):
    a = a.reshape(b, s // CMP_STRIDE, CMP_STRIDE, N_KV, HEAD_DIM).transpose(0, 3, 1, 2, 4)
    return a.reshape(b, N_KV, s // CMP_STRIDE, CMP_STRIDE * HEAD_DIM)


def kernel(x, norm1_g, w_in, rnn_conv_w, rnn_conv_b, rg_w_a, rg_b_a, rg_w_x, rg_b_x, lru_lambda, w_rnn_out, q_norm_g, k_cmp_norm_g, k_slc_norm_g, k_win_norm_g, cmp_pe_k, cmp_w1_k, cmp_w2_k, cmp_pe_v, cmp_w1_v, cmp_w2_v, w_attn_out, w_out, norm2_g, w_up, ffn_conv_w, ffn_conv_b, w_down):
    b, s, d = x.shape
    assert d == D_MODEL and s % SEL_TRIP == 0 and s // SEL_LEN <= SEL_LANES
    t = b * s
    x2 = x.reshape(t, d)

    rx, rgate, kvc, gates, mg, qpad, kaug_t, vslc, kwin_t, vwin = _in_proj(
        x2, b, s, norm1_g, w_in, q_norm_g, k_slc_norm_g, k_win_norm_g)

    y_rnn = _rglru(rx.reshape(b, s, D_RNN), rgate.reshape(b, s, D_RNN), rnn_conv_w, rnn_conv_b,
                   rg_w_a, rg_b_a, rg_w_x, rg_b_x, lru_lambda, w_rnn_out).reshape(t, d)

    kcmp, vcmp = _compress(_cmp_rows(kvc[:, :KV_W], b, s), _cmp_rows(kvc[:, KV_W:], b, s),
                           cmp_pe_k, cmp_pe_v, cmp_w1_k, cmp_w1_v, cmp_w2_k, cmp_w2_v, k_cmp_norm_g)
    kcmp_pad = jnp.pad(kcmp.astype(BF16), ((0, 0), (0, 0), (0, 0), (0, QK_LANES - HEAD_DIM)))
    vcmp_t = vcmp.transpose(0, 1, 3, 2).astype(BF16)
    flag = _static_flag(q_norm_g, k_cmp_norm_g, k_slc_norm_g, k_win_norm_g)
    o_attn = _attention_t(flag, qpad, gates.reshape(b, s, N_KV * GATE_LANES), kcmp_pad, vcmp_t,
                          _overlap_matrix_t(s), kaug_t, vslc, kwin_t, vwin).reshape(t, d)

    x1, hn = _merge(x2, y_rnn, o_attn, mg, w_attn_out, w_out, norm2_g)
    act = _ffn_up(hn, w_up, ffn_conv_w, ffn_conv_b, s)
    return _ffn_down(x1, act, w_down).reshape(b, s, d)
```

```python
import functools

import numpy as np
import jax
import jax.numpy as jnp
from jax import lax
from jax.experimental import pallas as pl
from jax.experimental.pallas import tpu as pltpu

F32 = jnp.float32
BF16 = jnp.bfloat16

D_MODEL = 1024
D_RNN = 1344
RNN_BLOCKS = 16
RNN_CONV_W = 4
LRU_C = 8.0
N_HEADS = 16
N_KV = 4
HEAD_DIM = 64
GROUP = N_HEADS // N_KV
KV_W = N_KV * HEAD_DIM
CMP_LEN = 32
CMP_STRIDE = 16
CMP_HIDDEN = 128
SEL_LEN = 64
N_SELECT = 16
N_FORCED = 3
WINDOW = 512
D_FF = 3 * D_MODEL
FFN_CONV_W = 3
EPS = 1e-6
NEG = -1e30
FORCE = 1e6
SCALE = HEAD_DIM ** -0.5

SEL_LANES = 128
V_LANES = 128
QK_LANES = 128
GATE_LANES = 128
TQ = 256
SEL_TRIP = 1024
SEL_ARM_ROWS = 32
LOG2E = float(np.log2(np.e))
HALO = 8
VMEM_LIMIT = 56 * 1024 * 1024
STATIC_SCORE_BOUND = 40.0


def _cparams(*sem):
    return pltpu.CompilerParams(dimension_semantics=sem, vmem_limit_bytes=VMEM_LIMIT)


def _gelu(x):
    c = float(np.sqrt(2.0 / np.pi))
    half = 0.5 * x
    return half + half * jnp.tanh(x * (c + (c * 0.044715) * (x * x)))


def _sigmoid(x):
    return 0.5 + 0.5 * jnp.tanh(0.5 * x)


def _rms(x, g):
    ms = jnp.mean(x * x, axis=-1, keepdims=True)
    return x * lax.rsqrt(ms + EPS) * g


def _dot(a, b):
    return jnp.dot(a, b, preferred_element_type=F32)


def _dot_nt(a, b):
    return lax.dot_general(a, b, (((1,), (1,)), ((), ())), preferred_element_type=F32)


def _split_bf16(x):
    hi = x.astype(BF16)
    lo = (x - hi.astype(F32)).astype(BF16)
    return hi, lo


def _tile(n, want):
    t = min(n, want)
    while n % t:
        t //= 2
    return t


def _full(shape):
    nd = len(shape)
    return pl.BlockSpec(shape, lambda *_: (0,) * nd)


def _in_proj_kernel(x_ref, g_ref, w_rx, w_rg, w_q, w_kv, w_g, w_mg, qg_ref, gs_ref, gw_ref, eye_ref,
                    kw0_ref, vw0_ref,
                    o_rx, o_rg, o_cmp, o_g, o_mg, q_ref, ka_ref, vs_ref, kw_ref, vw_ref, *, tiles_per_seq):
    del kw0_ref, vw0_ref
    h = _rms(x_ref[...], g_ref[...]).astype(BF16)

    q = _dot(h, w_q[...])
    kv = _dot(h, w_kv[...])
    o_rx[...] = _dot(h, w_rx[...])
    for g in range(N_KV):
        for r in range(GROUP):
            lo = (g * GROUP + r) * HEAD_DIM
            y = _rms(q[:, lo:lo + HEAD_DIM], qg_ref[...]) * (SCALE * LOG2E)
            q_ref[0, g, 0, :, r * TQ:(r + 1) * TQ] = _dot_nt(eye_ref[...], y.astype(BF16)).astype(BF16)

    o_rg[...] = _dot(h, w_rg[...])
    o_mg[...] = _dot(h, w_mg[...])
    o_cmp[...] = kv[:, :2 * KV_W]
    pos = (pl.program_id(0) % tiles_per_seq) * TQ + lax.broadcasted_iota(jnp.int32, (TQ, 1), 0)
    blk = lax.shift_right_logical(pos, SEL_LEN.bit_length() - 1)
    onehot = (lax.broadcasted_iota(jnp.int32, (1, SEL_LANES), 1) == blk).astype(BF16)
    zeros = jnp.zeros((TQ, QK_LANES - HEAD_DIM), BF16)
    ones = jnp.ones((V_LANES - HEAD_DIM, TQ), BF16)
    for g in range(N_KV):
        lo = g * HEAD_DIM
        ka_ref[0, g, :, 0:SEL_LANES] = onehot
        ka_ref[0, g, :, SEL_LANES:SEL_LANES + HEAD_DIM] = _rms(
            kv[:, 2 * KV_W + lo:2 * KV_W + lo + HEAD_DIM], gs_ref[...]).astype(BF16)
        ka_ref[0, g, :, SEL_LANES + HEAD_DIM:SEL_LANES + QK_LANES] = zeros
        kw_ref[0, g, :, 0:HEAD_DIM] = _rms(kv[:, 4 * KV_W + lo:4 * KV_W + lo + HEAD_DIM], gw_ref[...]).astype(BF16)
        kw_ref[0, g, :, HEAD_DIM:QK_LANES] = zeros
        for src, dst in ((3, vs_ref), (5, vw_ref)):
            v = kv[:, src * KV_W + lo:src * KV_W + lo + HEAD_DIM].astype(BF16)
            dst[0, g, 0:HEAD_DIM, :] = _dot_nt(eye_ref[0:HEAD_DIM, :], v).astype(BF16)
            dst[0, g, HEAD_DIM:V_LANES, :] = ones

    o_g[...] = _dot(h, w_g[...])


def _in_proj(x2, b, s, norm1_g, w_in, q_g, g_slc, g_win):
    t = x2.shape[0]
    nq = s // TQ
    pad_blocks = WINDOW // TQ
    kvw = 6 * KV_W
    o0 = 2 * D_RNN
    o1 = o0 + N_HEADS * HEAD_DIM
    o2 = o1 + kvw
    o3 = o2 + 3 * N_HEADS
    wb = w_in.astype(BF16)
    w_g = jnp.pad(wb[:, o2:o3].reshape(D_MODEL, N_KV, 3 * GROUP), ((0, 0), (0, 0), (0, GATE_LANES - 3 * GROUP)))
    w_g = w_g.reshape(D_MODEL, N_KV * GATE_LANES)
    ws = [wb[:, :D_RNN], wb[:, D_RNN:o0], wb[:, o0:o1], wb[:, o1:o2], w_g, wb[:, o3:]]
    eye = jnp.asarray(np.eye(QK_LANES, HEAD_DIM), BF16)
    gain = lambda v: v.reshape(1, HEAD_DIM)
    rows = lambda n: pl.BlockSpec((TQ, n), lambda i: (i, 0))
    flat = lambda n: jax.ShapeDtypeStruct((t, n), F32)
    kwin0 = jnp.zeros((b, N_KV, s + WINDOW, QK_LANES), BF16)
    vwin0 = jnp.zeros((b, N_KV, V_LANES, s + WINDOW), BF16)
    n_in = 14
    return pl.pallas_call(
        functools.partial(_in_proj_kernel, tiles_per_seq=nq),
        grid=(t // TQ,),
        in_specs=[rows(D_MODEL), _full((1, D_MODEL))] + [_full(w.shape) for w in ws]
        + [_full((1, HEAD_DIM))] * 3 + [_full((QK_LANES, HEAD_DIM))]
        + [pl.BlockSpec(memory_space=pl.ANY)] * 2,
        out_specs=[rows(D_RNN), rows(D_RNN), rows(2 * KV_W), rows(N_KV * GATE_LANES), rows(2 * D_MODEL),
                   pl.BlockSpec((1, N_KV, 1, QK_LANES, GROUP * TQ), lambda i: (i // nq, 0, i % nq, 0, 0)),
                   pl.BlockSpec((1, N_KV, TQ, SEL_LANES + QK_LANES), lambda i: (i // nq, 0, i % nq, 0)),
                   pl.BlockSpec((1, N_KV, V_LANES, TQ), lambda i: (i // nq, 0, 0, i % nq)),
                   pl.BlockSpec((1, N_KV, TQ, QK_LANES), lambda i: (i // nq, 0, i % nq + pad_blocks, 0)),
                   pl.BlockSpec((1, N_KV, V_LANES, TQ), lambda i: (i // nq, 0, 0, i % nq + pad_blocks))],
        out_shape=[flat(D_RNN), flat(D_RNN), flat(2 * KV_W), flat(N_KV * GATE_LANES), flat(2 * D_MODEL),
                   jax.ShapeDtypeStruct((b, N_KV, nq, QK_LANES, GROUP * TQ), BF16),
                   jax.ShapeDtypeStruct((b, N_KV, s, SEL_LANES + QK_LANES), BF16),
                   jax.ShapeDtypeStruct((b, N_KV, V_LANES, s), BF16),
                   jax.ShapeDtypeStruct(kwin0.shape, BF16), jax.ShapeDtypeStruct(vwin0.shape, BF16)],
        input_output_aliases={n_in - 2: 8, n_in - 1: 9},
        compiler_params=_cparams("parallel"),
        name="in_proj",
    )(x2, norm1_g.reshape(1, D_MODEL), *ws, gain(q_g), gain(g_slc), gain(g_win), eye, kwin0, vwin0)


def _rglru_kernel(rx_ref, rg_ref, cw_ref, cb_ref, wa_ref, ba_ref, wx_ref, bx_ref, lam_ref, wo_ref,
                  y_ref, xs_ref, carry_ref, *, tt):
    ti = pl.program_id(1)

    @pl.when(ti == 0)
    def _():
        xs_ref[0:HALO, :] = jnp.zeros((HALO, D_RNN), F32)
        carry_ref[...] = jnp.zeros((HALO, D_RNN), F32)

    xs_ref[HALO:HALO + tt, :] = rx_ref[0]
    cw = cw_ref[...]
    x = cb_ref[...] + cw[RNN_CONV_W - 1:RNN_CONV_W, :] * xs_ref[HALO:HALO + tt, :]
    for k in range(RNN_CONV_W - 1):
        off = HALO - (RNN_CONV_W - 1) + k
        x = x + cw[k:k + 1, :] * xs_ref[off:off + tt, :]
    xs_ref[0:HALO, :] = xs_ref[tt:tt + HALO, :]

    xb = x.astype(BF16)
    r = _sigmoid(_dot(xb, wa_ref[...]) + ba_ref[...])
    gi = _sigmoid(_dot(xb, wx_ref[...]) + bx_ref[...])
    lam = lam_ref[...]
    log_sig = jnp.minimum(lam, 0.0) - jnp.log1p(jnp.exp(-jnp.abs(lam)))
    log_a = LRU_C * r * log_sig
    a = jnp.exp(log_a)
    y = 1.0 - a * a
    u = jnp.where(y > 0.0, y * lax.rsqrt(y), 0.0) * (gi * x)

    a = a.reshape(tt // HALO, HALO, D_RNN)
    u = u.reshape(tt // HALO, HALO, D_RNN)
    sub = lax.broadcasted_iota(jnp.int32, (1, HALO, 1), 1)
    for d in (1, 2, 4):
        keep = sub >= d
        a_sh = pltpu.roll(a, d, 1)
        u_sh = pltpu.roll(u, d, 1)
        u = jnp.where(keep, a * u_sh + u, u)
        a = jnp.where(keep, a * a_sh, a)
    carry = carry_ref[...]
    hs = []
    for g in range(tt // HALO):
        h = a[g] * carry + u[g]
        hs.append(h)
        carry = jnp.broadcast_to(h[HALO - 1:HALO, :], (HALO, D_RNN))
    carry_ref[...] = carry

    hg = (jnp.concatenate(hs, axis=0) * _gelu(rg_ref[0])).astype(BF16)
    y_ref[0] = _dot(hg, wo_ref[...])


def _block_diag(w):
    n, c, d = w.shape
    return jnp.einsum("ncd,nm->ncmd", w, jnp.eye(n, dtype=w.dtype)).reshape(n * c, n * d)


def _rglru(rx3, rg3, conv_w, conv_b, w_a, b_a, w_x, b_x, lam, w_out):
    b, s, _ = rx3.shape
    tt = _tile(s, 256)
    row = lambda v: v.reshape(1, D_RNN)
    return pl.pallas_call(
        functools.partial(_rglru_kernel, tt=tt),
        grid=(b, s // tt),
        in_specs=[pl.BlockSpec((1, tt, D_RNN), lambda i, j: (i, j, 0)),
                  pl.BlockSpec((1, tt, D_RNN), lambda i, j: (i, j, 0)),
                  _full((RNN_CONV_W, D_RNN)), _full((1, D_RNN)),
                  _full((D_RNN, D_RNN)), _full((1, D_RNN)),
                  _full((D_RNN, D_RNN)), _full((1, D_RNN)),
                  _full((1, D_RNN)), _full((D_RNN, D_MODEL))],
        out_specs=pl.BlockSpec((1, tt, D_MODEL), lambda i, j: (i, j, 0)),
        out_shape=jax.ShapeDtypeStruct((b, s, D_MODEL), F32),
        scratch_shapes=[pltpu.VMEM((tt + HALO, D_RNN), F32), pltpu.VMEM((HALO, D_RNN), F32)],
        compiler_params=_cparams("arbitrary", "arbitrary"),
        name="rglru",
    )(rx3, rg3, conv_w, row(conv_b), _block_diag(w_a).astype(BF16), row(b_a),
      _block_diag(w_x).astype(BF16), row(b_x), row(lam), w_out.astype(BF16))


def _compress_kernel(rk_ref, rv_ref, pek_ref, pev_ref, w1k_ref, w1v_ref, w2k_ref, w2v_ref, gk_ref,
                     kc_ref, vc_ref):
    half = CMP_STRIDE * HEAD_DIM
    ncp = rk_ref.shape[2]

    def mlp(r_ref, pe_ref, w1_ref, w2_ref):
        r = r_ref[0, 0]
        pe = pe_ref[...]
        lo = _dot((r + pe[:, :half]).astype(BF16), w1_ref[:half, :])
        hi = _dot((r + pe[:, half:]).astype(BF16), w1_ref[half:, :])
        hid = lo + pltpu.roll(hi, ncp - 1, 0)
        return _dot(_gelu(hid).astype(BF16), w2_ref[...])

    kc_ref[0, 0] = _rms(mlp(rk_ref, pek_ref, w1k_ref, w2k_ref), gk_ref[...])
    vc_ref[0, 0] = mlp(rv_ref, pev_ref, w1v_ref, w2v_ref)


def _compress(rk, rv, pe_k, pe_v, w1k, w1v, w2k, w2v, g_cmp):
    b, g, ncp, wide = rk.shape
    blk = pl.BlockSpec((1, 1, ncp, wide), lambda i, j: (i, j, 0, 0))
    out = pl.BlockSpec((1, 1, ncp, HEAD_DIM), lambda i, j: (i, j, 0, 0))
    w1 = _full((CMP_LEN * HEAD_DIM, CMP_HIDDEN))
    w2 = _full((CMP_HIDDEN, HEAD_DIM))
    pe = _full((1, CMP_LEN * HEAD_DIM))
    return pl.pallas_call(
        _compress_kernel,
        grid=(b, g),
        in_specs=[blk, blk, pe, pe, w1, w1, w2, w2, _full((1, HEAD_DIM))],
        out_specs=[out, out],
        out_shape=[jax.ShapeDtypeStruct((b, g, ncp, HEAD_DIM), F32)] * 2,
        compiler_params=_cparams("parallel", "parallel"),
        name="compress",
    )(rk, rv, pe_k.reshape(1, -1), pe_v.reshape(1, -1), w1k.astype(BF16), w1v.astype(BF16),
      w2k.astype(BF16), w2v.astype(BF16), g_cmp.reshape(1, HEAD_DIM))


def _exp_terms(sm, static):
    if static:
        return jnp.exp2(sm)
    return jnp.exp2(sm - jnp.max(sm, axis=0, keepdims=True))


def _normalise(acc):
    return acc[:HEAD_DIM] * (1.0 / acc[HEAD_DIM:])


def _attn_kernel(flag_ref, q_ref, gt_ref, kc_ref, vct_ref, ovt_ref, ka_ref, vst_ref, kw_ref, vwt_ref, wb_ref,
                   o_ref):
    qi = pl.program_id(2)
    st = qi * TQ
    cols = GROUP * TQ
    ncp = kc_ref.shape[2]
    static_ok = flag_ref[0, 0] != 0
    both = lambda f: lax.cond(static_ok, lambda: f(True), lambda: f(False))

    qt = q_ref[0, 0, 0]
    q_loc = lax.broadcasted_iota(jnp.int32, (1, cols), 1) & (TQ - 1)
    t_col = st + q_loc

    def compressed(static):
        def over(n):
            cidx = lax.broadcasted_iota(jnp.int32, (n, 1), 0)
            valid = (cidx * CMP_STRIDE + (CMP_LEN - 1)) <= t_col
            sm = jnp.where(valid, _dot(kc_ref[0, 0, 0:n, :], qt), NEG)
            e = _exp_terms(sm, True) if static else jnp.where(valid, _exp_terms(sm, False), 0.0)
            l = jnp.sum(e, axis=0, keepdims=True)
            p = e * (1.0 / jnp.where(l > 0.0, l, 1.0))
            psum = p[:, 0:TQ] + p[:, TQ:2 * TQ] + p[:, 2 * TQ:3 * TQ] + p[:, 3 * TQ:4 * TQ]
            if n < ncp:
                psum = jnp.concatenate([psum, jnp.zeros((ncp - n, TQ), F32)], axis=0)
            return _dot(vct_ref[0, 0, :, 0:n], p.astype(BF16)), psum

        half = ncp // 2
        if half % 128:
            return over(ncp)
        return lax.cond(st + TQ <= half * CMP_STRIDE + CMP_LEN - 1, lambda: over(half), lambda: over(ncp))

    o_cmp, psum = both(compressed)

    p_hi, p_lo = _split_bf16(psum)
    imp_t = _dot(ovt_ref[...], p_hi) + _dot(ovt_ref[...], p_lo)
    cur = lax.shift_right_logical(st + lax.broadcasted_iota(jnp.int32, (1, TQ), 1), SEL_LEN.bit_length() - 1)

    def select(nb):
        jb = lax.broadcasted_iota(jnp.int32, (nb, 1), 0)
        jbf = jb.astype(F32)
        forced = (jb == 0) | (jb == cur) | (jb == cur - 1)
        sel = forced
        work = jnp.where((jb <= cur) & jnp.logical_not(forced), imp_t[:nb], -FORCE)
        for _ in range(N_SELECT - N_FORCED):
            mx = jnp.max(work, axis=0, keepdims=True)
            first = jnp.min(jnp.where(work == mx, jbf, float(SEL_LANES)), axis=0, keepdims=True)
            pick = jbf == first
            sel = sel | pick
            work = jnp.where(pick, -jnp.inf, work)
        bias = jnp.where(sel & (jb <= cur), 0.0, NEG)
        if nb < SEL_LANES:
            bias = jnp.concatenate([bias, jnp.full((SEL_LANES - nb, TQ), NEG, F32)], axis=0)
        return bias.astype(BF16)

    tiles_per_arm = SEL_ARM_ROWS * SEL_LEN // TQ
    arms = [functools.partial(select, SEL_ARM_ROWS * (i + 1)) for i in range(SEL_LANES // SEL_ARM_ROWS)]
    bias = lax.switch(jnp.minimum(lax.div(qi, tiles_per_arm), len(arms) - 1), arms)
    qaug = jnp.concatenate([jnp.concatenate([bias] * GROUP, axis=1), qt], axis=0)

    n_full = lax.shift_right_logical(st, SEL_TRIP.bit_length() - 1)
    causal = jnp.where(lax.broadcasted_iota(jnp.int32, (TQ, 1), 0) <= q_loc, 0.0, NEG)

    def selected(static):
        def scores(keys):
            return _dot(ka_ref[0, 0, keys, :], qaug), vst_ref[0, 0, :, keys]

        def last(before):
            s_t, v_t = scores(pl.ds(pl.multiple_of(st - before, TQ), before + TQ))
            own = s_t[before:] + causal
            s_t = jnp.concatenate([s_t[:before], own], axis=0) if before else own
            if static:
                return _dot(v_t, _exp_terms(s_t, True).astype(BF16))
            m = jnp.max(s_t, axis=0, keepdims=True)
            return m, _dot(v_t, jnp.exp2(s_t - m).astype(BF16))

        rem = lax.shift_right_logical(st & (SEL_TRIP - 1), TQ.bit_length() - 1)
        first = lax.switch(rem, [functools.partial(last, i * TQ) for i in range(SEL_TRIP // TQ)])
        trip_keys = lambda c: pl.ds(pl.multiple_of(c * SEL_TRIP, SEL_TRIP), SEL_TRIP)
        if static:
            def trip(c, acc):
                s_t, v_t = scores(trip_keys(c))
                return acc + _dot(v_t, jnp.exp2(s_t).astype(BF16))

            return lax.fori_loop(0, n_full, trip, first)

        def trip(c, carry):
            m, acc = carry
            s_t, v_t = scores(trip_keys(c))
            mn = jnp.maximum(m, jnp.max(s_t, axis=0, keepdims=True))
            return mn, jnp.exp2(m - mn) * acc + _dot(v_t, jnp.exp2(s_t - mn).astype(BF16))

        return lax.fori_loop(0, n_full, trip, first)[1]

    def window(static):
        wkeys = pl.ds(pl.multiple_of(st, TQ), WINDOW + TQ)
        s_t = _dot(kw_ref[0, 0, wkeys, :], qt)
        e = jnp.concatenate([_exp_terms(s_t[:, r * TQ:(r + 1) * TQ] + wb_ref[0], static)
                             for r in range(GROUP)], axis=1)
        return _dot(vwt_ref[0, 0, :, wkeys], e.astype(BF16))

    o_slc = _normalise(both(selected))
    o_win = _normalise(both(window))

    sg = _sigmoid(gt_ref[0]).T
    heads = []
    for r in range(GROUP):
        acc = jnp.zeros((HEAD_DIM, TQ), F32)
        for j, y in enumerate((o_cmp, o_slc, o_win)):
            acc = acc + sg[3 * r + j:3 * r + j + 1, :] * y[:, r * TQ:(r + 1) * TQ]
        heads.append(acc)
    o_ref[0] = jnp.concatenate(heads, axis=0).T


def _attention(flag, q_t, gates3, kcmp, vcmp_t, overlap_t, kaug, vslc_t, kwin, vwin_t):
    b, _, n_tiles, _, cols = q_t.shape
    s = n_tiles * TQ
    ncp = kcmp.shape[2]
    sp = kwin.shape[2]
    per_bg = lambda shape: pl.BlockSpec((1, 1) + shape, lambda i, j, k: (i, j, 0, 0))
    return pl.pallas_call(
        _attn_kernel,
        grid=(b, N_KV, n_tiles),
        in_specs=[pl.BlockSpec(memory_space=pltpu.SMEM),
                  pl.BlockSpec((1, 1, 1, QK_LANES, cols), lambda i, j, k: (i, j, k, 0, 0)),
                  pl.BlockSpec((1, TQ, GATE_LANES), lambda i, j, k: (i, k, j)),
                  per_bg((ncp, QK_LANES)), per_bg((HEAD_DIM, ncp)), _full((SEL_LANES, ncp)),
                  per_bg((s, SEL_LANES + QK_LANES)), per_bg((V_LANES, s)),
                  per_bg((sp, QK_LANES)), per_bg((V_LANES, sp)),
                  pl.BlockSpec((1, WINDOW + TQ, TQ), lambda i, j, k: (jnp.minimum(k, WINDOW // TQ), 0, 0))],
        out_specs=pl.BlockSpec((1, TQ, KV_W), lambda i, j, k: (i, k, j)),
        out_shape=jax.ShapeDtypeStruct((b, s, N_HEADS * HEAD_DIM), F32),
        compiler_params=_cparams("parallel", "parallel", "arbitrary"),
        name="nsa_attention",
    )(flag, q_t, gates3, kcmp, vcmp_t, overlap_t, kaug, vslc_t, kwin, vwin_t, _band_biases())


def _band_biases():
    q = np.arange(TQ)[None, :]
    kw = np.arange(WINDOW + TQ)[:, None] - WINDOW
    in_band = (kw <= q) & (kw > q - WINDOW)
    bands = [np.where(in_band & (kw + tile * TQ >= 0), 0.0, NEG) for tile in range(WINDOW // TQ + 1)]
    return jnp.asarray(np.stack(bands), F32)


def _overlap_matrix_t(s):
    ncp = s // CMP_STRIDE
    i = np.arange(ncp)[None, :]
    j = np.arange(SEL_LANES)[:, None]
    lo = np.maximum(i * CMP_STRIDE, j * SEL_LEN)
    hi = np.minimum(i * CMP_STRIDE + CMP_LEN, (j + 1) * SEL_LEN)
    ov = np.maximum(hi - lo, 0) / CMP_STRIDE
    ov = ov * (i < (s - CMP_LEN) // CMP_STRIDE + 1) * (j < s // SEL_LEN)
    return jnp.asarray(ov, dtype=BF16)


def _static_flag(q_g, *k_gs):
    kmax = functools.reduce(jnp.maximum, [jnp.max(jnp.abs(g)) for g in k_gs])
    bound = HEAD_DIM * SCALE * jnp.max(jnp.abs(q_g)) * kmax
    return (bound < STATIC_SCORE_BOUND).astype(jnp.int32).reshape(1, 1)


def _merge_kernel(x_ref, yr_ref, o_ref, mg_ref, wa_ref, wo_ref, g2_ref, x1_ref, hn_ref):
    y_attn = _dot(o_ref[...].astype(BF16), wa_ref[...])
    mg = mg_ref[...]
    m = _sigmoid(mg[:, :D_MODEL]) * yr_ref[...] + _sigmoid(mg[:, D_MODEL:]) * y_attn
    x1 = x_ref[...] + _dot(m.astype(BF16), wo_ref[...])
    x1_ref[...] = x1
    hn_ref[...] = _rms(x1, g2_ref[...]).astype(BF16)


def _merge(x2, y_rnn, o_attn, mg, w_attn_out, w_out, norm2_g):
    t = x2.shape[0]
    tm = _tile(t, 512)
    rowblk = lambda n: pl.BlockSpec((tm, n), lambda i: (i, 0))
    return pl.pallas_call(
        _merge_kernel,
        grid=(t // tm,),
        in_specs=[rowblk(D_MODEL), rowblk(D_MODEL), rowblk(D_MODEL), rowblk(2 * D_MODEL),
                  _full((D_MODEL, D_MODEL)), _full((D_MODEL, D_MODEL)), _full((1, D_MODEL))],
        out_specs=[rowblk(D_MODEL), rowblk(D_MODEL)],
        out_shape=[jax.ShapeDtypeStruct((t, D_MODEL), F32), jax.ShapeDtypeStruct((t, D_MODEL), BF16)],
        compiler_params=_cparams("parallel"),
        name="merge",
    )(x2, y_rnn, o_attn, mg, w_attn_out.astype(BF16), w_out.astype(BF16), norm2_g.reshape(1, D_MODEL))


def _ffn_up_kernel(hp_ref, h_ref, wg_ref, wv_ref, cwg_ref, cwv_ref, cbg_ref, cbv_ref, act_ref, up_ref,
                   *, tm, tiles_per_seq):
    first = (pl.program_id(0) % tiles_per_seq) == 0
    keep = jnp.where(first, 0.0, 1.0)

    def conv(w_ref, cw_ref, cb_ref):
        up_ref[0:HALO, :] = _dot(hp_ref[...], w_ref[...]) * keep
        up_ref[HALO:HALO + tm, :] = _dot(h_ref[...], w_ref[...])
        cw = cw_ref[...]
        y = cb_ref[...] + cw[FFN_CONV_W - 1:FFN_CONV_W, :] * up_ref[HALO:HALO + tm, :]
        for k in range(FFN_CONV_W - 1):
            off = HALO - (FFN_CONV_W - 1) + k
            y = y + cw[k:k + 1, :] * up_ref[off:off + tm, :]
        return y

    gate = conv(wg_ref, cwg_ref, cbg_ref)
    val = conv(wv_ref, cwv_ref, cbv_ref)
    act_ref[...] = (_gelu(gate) * val).astype(BF16)


def _ffn_up(hn, w_up, conv_w, conv_b, s):
    t = hn.shape[0]
    tm = _tile(s, 1024)
    tn = 512
    nj = D_FF // tn
    wb = w_up.astype(BF16)
    cb = conv_b.reshape(1, 2 * D_FF)
    return pl.pallas_call(
        functools.partial(_ffn_up_kernel, tm=tm, tiles_per_seq=s // tm),
        grid=(t // tm, nj),
        in_specs=[pl.BlockSpec((HALO, D_MODEL), lambda i, j: (jnp.maximum(i * (tm // HALO) - 1, 0), 0)),
                  pl.BlockSpec((tm, D_MODEL), lambda i, j: (i, 0)),
                  pl.BlockSpec((D_MODEL, tn), lambda i, j: (0, j)),
                  pl.BlockSpec((D_MODEL, tn), lambda i, j: (0, j + nj)),
                  pl.BlockSpec((FFN_CONV_W, tn), lambda i, j: (0, j)),
                  pl.BlockSpec((FFN_CONV_W, tn), lambda i, j: (0, j + nj)),
                  pl.BlockSpec((1, tn), lambda i, j: (0, j)),
                  pl.BlockSpec((1, tn), lambda i, j: (0, j + nj))],
        out_specs=pl.BlockSpec((tm, tn), lambda i, j: (i, j)),
        out_shape=jax.ShapeDtypeStruct((t, D_FF), BF16),
        scratch_shapes=[pltpu.VMEM((tm + HALO, tn), F32)],
        compiler_params=_cparams("parallel", "arbitrary"),
        name="ffn_up",
    )(hn, hn, wb, wb, conv_w, conv_w, cb, cb)


def _ffn_down_kernel(x1_ref, act_ref, w_ref, o_ref):
    o_ref[...] = x1_ref[...] + _dot(act_ref[...], w_ref[...])


def _ffn_down(x1, act, w_down):
    t = x1.shape[0]
    tm = _tile(t, 512)
    return pl.pallas_call(
        _ffn_down_kernel,
        grid=(t // tm,),
        in_specs=[pl.BlockSpec((tm, D_MODEL), lambda i: (i, 0)), pl.BlockSpec((tm, D_FF), lambda i: (i, 0)),
                  _full((D_FF, D_MODEL))],
        out_specs=pl.BlockSpec((tm, D_MODEL), lambda i: (i, 0)),
        out_shape=jax.ShapeDtypeStruct((t, D_MODEL), F32),
        compiler_params=_cparams("parallel"),
        name="ffn_down",
    )(x1, act, w_down.astype(BF16))


def _cmp_rows(a, b, s):
    a = a.reshape(b, s // CMP_STRIDE, CMP_STRIDE, N_KV, HEAD_DIM).transpose(0, 3, 1, 2, 4)
    return a.reshape(b, N_KV, s // CMP_STRIDE, CMP_STRIDE * HEAD_DIM)


def kernel(x, norm1_g, w_in, rnn_conv_w, rnn_conv_b, rg_w_a, rg_b_a, rg_w_x, rg_b_x, lru_lambda, w_rnn_out, q_norm_g, k_cmp_norm_g, k_slc_norm_g, k_win_norm_g, cmp_pe_k, cmp_w1_k, cmp_w2_k, cmp_pe_v, cmp_w1_v, cmp_w2_v, w_attn_out, w_out, norm2_g, w_up, ffn_conv_w, ffn_conv_b, w_down):
    b, s, d = x.shape
    assert d == D_MODEL and s % SEL_TRIP == 0 and s // SEL_LEN <= SEL_LANES
    t = b * s
    x2 = x.reshape(t, d)

    rx, rgate, kvc, gates, mg, qpad, kaug_t, vslc, kwin_t, vwin = _in_proj(
        x2, b, s, norm1_g, w_in, q_norm_g, k_slc_norm_g, k_win_norm_g)

    y_rnn = _rglru(rx.reshape(b, s, D_RNN), rgate.reshape(b, s, D_RNN), rnn_conv_w, rnn_conv_b,
                   rg_w_a, rg_b_a, rg_w_x, rg_b_x, lru_lambda, w_rnn_out).reshape(t, d)

    kcmp, vcmp = _compress(_cmp_rows(kvc[:, :KV_W], b, s), _cmp_rows(kvc[:, KV_W:], b, s),
                           cmp_pe_k, cmp_pe_v, cmp_w1_k, cmp_w1_v, cmp_w2_k, cmp_w2_v, k_cmp_norm_g)
    kcmp_pad = jnp.pad(kcmp.astype(BF16), ((0, 0), (0, 0), (0, 0), (0, QK_LANES - HEAD_DIM)))
    vcmp_t = vcmp.transpose(0, 1, 3, 2).astype(BF16)
    flag = _static_flag(q_norm_g, k_cmp_norm_g, k_slc_norm_g, k_win_norm_g)
    o_attn = _attention(flag, qpad, gates.reshape(b, s, N_KV * GATE_LANES), kcmp_pad, vcmp_t,
                          _overlap_matrix_t(s), kaug_t, vslc, kwin_t, vwin).reshape(t, d)

    x1, hn = _merge(x2, y_rnn, o_attn, mg, w_attn_out, w_out, norm2_g)
    act = _ffn_up(hn, w_up, ffn_conv_w, ffn_conv_b, s)
    return _ffn_down(x1, act, w_down).reshape(b, s, d)
```

```python
import functools

import numpy as np
import jax
import jax.numpy as jnp
from jax import lax
from jax.experimental import pallas as pl
from jax.experimental.pallas import tpu as pltpu

F32 = jnp.float32
BF16 = jnp.bfloat16

D_MODEL = 1024
D_RNN = 1344
RNN_BLOCKS = 16
RNN_CONV_W = 4
LRU_C = 8.0
N_HEADS = 16
N_KV = 4
HEAD_DIM = 64
GROUP = N_HEADS // N_KV
KV_W = N_KV * HEAD_DIM
CMP_LEN = 32
CMP_STRIDE = 16
CMP_HIDDEN = 128
SEL_LEN = 64
N_SELECT = 16
N_FORCED = 3
WINDOW = 512
D_FF = 3 * D_MODEL
FFN_CONV_W = 3
EPS = 1e-6
NEG = -1e30
FORCE = 1e6
SCALE = HEAD_DIM ** -0.5

SEL_LANES = 128
V_LANES = 128
QK_LANES = 128
GATE_LANES = 128
TQ = 256
SEL_TRIP = 1024
SEL_ARM_ROWS = 32
LOG2E = float(np.log2(np.e))
HALO = 8
VMEM_LIMIT = 56 * 1024 * 1024
STATIC_SCORE_BOUND = 40.0


def _cparams(*sem):
    return pltpu.CompilerParams(dimension_semantics=sem, vmem_limit_bytes=VMEM_LIMIT)


def _gelu(x):
    c = float(np.sqrt(2.0 / np.pi))
    half = 0.5 * x
    return half + half * jnp.tanh(x * (c + (c * 0.044715) * (x * x)))


def _sigmoid(x):
    return 0.5 + 0.5 * jnp.tanh(0.5 * x)


def _rms(x, g):
    ms = jnp.mean(x * x, axis=-1, keepdims=True)
    return x * lax.rsqrt(ms + EPS) * g


def _dot(a, b):
    return jnp.dot(a, b, preferred_element_type=F32)


def _dot_nt(a, b):
    return lax.dot_general(a, b, (((1,), (1,)), ((), ())), preferred_element_type=F32)


def _split_bf16(x):
    hi = x.astype(BF16)
    lo = (x - hi.astype(F32)).astype(BF16)
    return hi, lo


def _tile(n, want):
    t = min(n, want)
    while n % t:
        t //= 2
    return t


def _full(shape):
    nd = len(shape)
    return pl.BlockSpec(shape, lambda *_: (0,) * nd)


def _in_proj_kernel(x_ref, g_ref, w_rx, w_rg, w_q, w_kv, w_g, w_mg, qg_ref, gs_ref, gw_ref, eye_ref,
                    kw0_ref, vw0_ref,
                    o_rx, o_rg, o_cmp, o_g, o_mg, q_ref, ka_ref, vs_ref, kw_ref, vw_ref, *, tiles_per_seq):
    del kw0_ref, vw0_ref
    h = _rms(x_ref[...], g_ref[...]).astype(BF16)

    q = _dot(h, w_q[...])
    kv = _dot(h, w_kv[...])
    o_rx[...] = _dot(h, w_rx[...])
    for g in range(N_KV):
        for r in range(GROUP):
            lo = (g * GROUP + r) * HEAD_DIM
            y = _rms(q[:, lo:lo + HEAD_DIM], qg_ref[...]) * (SCALE * LOG2E)
            q_ref[0, g, 0, :, r * TQ:(r + 1) * TQ] = _dot_nt(eye_ref[...], y.astype(BF16)).astype(BF16)

    o_rg[...] = _dot(h, w_rg[...])
    o_mg[...] = _dot(h, w_mg[...])
    o_cmp[...] = kv[:, :2 * KV_W]
    pos = (pl.program_id(0) % tiles_per_seq) * TQ + lax.broadcasted_iota(jnp.int32, (TQ, 1), 0)
    blk = lax.shift_right_logical(pos, SEL_LEN.bit_length() - 1)
    onehot = (lax.broadcasted_iota(jnp.int32, (1, SEL_LANES), 1) == blk).astype(BF16)
    zeros = jnp.zeros((TQ, QK_LANES - HEAD_DIM), BF16)
    ones = jnp.ones((V_LANES - HEAD_DIM, TQ), BF16)
    for g in range(N_KV):
        lo = g * HEAD_DIM
        ka_ref[0, g, :, 0:SEL_LANES] = onehot
        ka_ref[0, g, :, SEL_LANES:SEL_LANES + HEAD_DIM] = _rms(
            kv[:, 2 * KV_W + lo:2 * KV_W + lo + HEAD_DIM], gs_ref[...]).astype(BF16)
        ka_ref[0, g, :, SEL_LANES + HEAD_DIM:SEL_LANES + QK_LANES] = zeros
        kw_ref[0, g, :, 0:HEAD_DIM] = _rms(kv[:, 4 * KV_W + lo:4 * KV_W + lo + HEAD_DIM], gw_ref[...]).astype(BF16)
        kw_ref[0, g, :, HEAD_DIM:QK_LANES] = zeros
        for src, dst in ((3, vs_ref), (5, vw_ref)):
            v = kv[:, src * KV_W + lo:src * KV_W + lo + HEAD_DIM].astype(BF16)
            dst[0, g, 0:HEAD_DIM, :] = _dot_nt(eye_ref[0:HEAD_DIM, :], v).astype(BF16)
            dst[0, g, HEAD_DIM:V_LANES, :] = ones

    o_g[...] = _dot(h, w_g[...])


def _in_proj(x2, b, s, norm1_g, w_in, q_g, g_slc, g_win):
    t = x2.shape[0]
    nq = s // TQ
    pad_blocks = WINDOW // TQ
    kvw = 6 * KV_W
    o0 = 2 * D_RNN
    o1 = o0 + N_HEADS * HEAD_DIM
    o2 = o1 + kvw
    o3 = o2 + 3 * N_HEADS
    wb = w_in.astype(BF16)
    w_g = jnp.pad(wb[:, o2:o3].reshape(D_MODEL, N_KV, 3 * GROUP), ((0, 0), (0, 0), (0, GATE_LANES - 3 * GROUP)))
    w_g = w_g.reshape(D_MODEL, N_KV * GATE_LANES)
    ws = [wb[:, :D_RNN], wb[:, D_RNN:o0], wb[:, o0:o1], wb[:, o1:o2], w_g, wb[:, o3:]]
    eye = jnp.asarray(np.eye(QK_LANES, HEAD_DIM), BF16)
    gain = lambda v: v.reshape(1, HEAD_DIM)
    rows = lambda n: pl.BlockSpec((TQ, n), lambda i: (i, 0))
    flat = lambda n: jax.ShapeDtypeStruct((t, n), F32)
    kwin0 = jnp.zeros((b, N_KV, s + WINDOW, QK_LANES), BF16)
    vwin0 = jnp.zeros((b, N_KV, V_LANES, s + WINDOW), BF16)
    n_in = 14
    return pl.pallas_call(
        functools.partial(_in_proj_kernel, tiles_per_seq=nq),
        grid=(t // TQ,),
        in_specs=[rows(D_MODEL), _full((1, D_MODEL))] + [_full(w.shape) for w in ws]
        + [_full((1, HEAD_DIM))] * 3 + [_full((QK_LANES, HEAD_DIM))]
        + [pl.BlockSpec(memory_space=pl.ANY)] * 2,
        out_specs=[rows(D_RNN), rows(D_RNN), rows(2 * KV_W), rows(N_KV * GATE_LANES), rows(2 * D_MODEL),
                   pl.BlockSpec((1, N_KV, 1, QK_LANES, GROUP * TQ), lambda i: (i // nq, 0, i % nq, 0, 0)),
                   pl.BlockSpec((1, N_KV, TQ, SEL_LANES + QK_LANES), lambda i: (i // nq, 0, i % nq, 0)),
                   pl.BlockSpec((1, N_KV, V_LANES, TQ), lambda i: (i // nq, 0, 0, i % nq)),
                   pl.BlockSpec((1, N_KV, TQ, QK_LANES), lambda i: (i // nq, 0, i % nq + pad_blocks, 0)),
                   pl.BlockSpec((1, N_KV, V_LANES, TQ), lambda i: (i // nq, 0, 0, i % nq + pad_blocks))],
        out_shape=[flat(D_RNN), flat(D_RNN), flat(2 * KV_W), flat(N_KV * GATE_LANES), flat(2 * D_MODEL),
                   jax.ShapeDtypeStruct((b, N_KV, nq, QK_LANES, GROUP * TQ), BF16),
                   jax.ShapeDtypeStruct((b, N_KV, s, SEL_LANES + QK_LANES), BF16),
                   jax.ShapeDtypeStruct((b, N_KV, V_LANES, s), BF16),
                   jax.ShapeDtypeStruct(kwin0.shape, BF16), jax.ShapeDtypeStruct(vwin0.shape, BF16)],
        input_output_aliases={n_in - 2: 8, n_in - 1: 9},
        compiler_params=_cparams("parallel"),
        name="in_proj",
    )(x2, norm1_g.reshape(1, D_MODEL), *ws, gain(q_g), gain(g_slc), gain(g_win), eye, kwin0, vwin0)


def _rglru_kernel(rx_ref, rg_ref, cw_ref, cb_ref, wa_ref, ba_ref, wx_ref, bx_ref, lam_ref, wo_ref,
                  y_ref, xs_ref, carry_ref, *, tt):
    ti = pl.program_id(1)

    @pl.when(ti == 0)
    def _():
        xs_ref[0:HALO, :] = jnp.zeros((HALO, D_RNN), F32)
        carry_ref[...] = jnp.zeros((HALO, D_RNN), F32)

    xs_ref[HALO:HALO + tt, :] = rx_ref[0]
    cw = cw_ref[...]
    x = cb_ref[...] + cw[RNN_CONV_W - 1:RNN_CONV_W, :] * xs_ref[HALO:HALO + tt, :]
    for k in range(RNN_CONV_W - 1):
        off = HALO - (RNN_CONV_W - 1) + k
        x = x + cw[k:k + 1, :] * xs_ref[off:off + tt, :]
    xs_ref[0:HALO, :] = xs_ref[tt:tt + HALO, :]

    xb = x.astype(BF16)
    r = _sigmoid(_dot(xb, wa_ref[...]) + ba_ref[...])
    gi = _sigmoid(_dot(xb, wx_ref[...]) + bx_ref[...])
    lam = lam_ref[...]
    log_sig = jnp.minimum(lam, 0.0) - jnp.log1p(jnp.exp(-jnp.abs(lam)))
    log_a = LRU_C * r * log_sig
    a = jnp.exp(log_a)
    y = 1.0 - a * a
    u = jnp.where(y > 0.0, y * lax.rsqrt(y), 0.0) * (gi * x)

    a = a.reshape(tt // HALO, HALO, D_RNN)
    u = u.reshape(tt // HALO, HALO, D_RNN)
    sub = lax.broadcasted_iota(jnp.int32, (1, HALO, 1), 1)
    for d in (1, 2, 4):
        keep = sub >= d
        a_sh = pltpu.roll(a, d, 1)
        u_sh = pltpu.roll(u, d, 1)
        u = jnp.where(keep, a * u_sh + u, u)
        a = jnp.where(keep, a * a_sh, a)
    carry = carry_ref[...]
    hs = []
    for g in range(tt // HALO):
        h = a[g] * carry + u[g]
        hs.append(h)
        carry = jnp.broadcast_to(h[HALO - 1:HALO, :], (HALO, D_RNN))
    carry_ref[...] = carry

    hg = (jnp.concatenate(hs, axis=0) * _gelu(rg_ref[0])).astype(BF16)
    y_ref[0] = _dot(hg, wo_ref[...])


def _block_diag(w):
    n, c, d = w.shape
    return jnp.einsum("ncd,nm->ncmd", w, jnp.eye(n, dtype=w.dtype)).reshape(n * c, n * d)


def _rglru(rx3, rg3, conv_w, conv_b, w_a, b_a, w_x, b_x, lam, w_out):
    b, s, _ = rx3.shape
    tt = _tile(s, 256)
    row = lambda v: v.reshape(1, D_RNN)
    return pl.pallas_call(
        functools.partial(_rglru_kernel, tt=tt),
        grid=(b, s // tt),
        in_specs=[pl.BlockSpec((1, tt, D_RNN), lambda i, j: (i, j, 0)),
                  pl.BlockSpec((1, tt, D_RNN), lambda i, j: (i, j, 0)),
                  _full((RNN_CONV_W, D_RNN)), _full((1, D_RNN)),
                  _full((D_RNN, D_RNN)), _full((1, D_RNN)),
                  _full((D_RNN, D_RNN)), _full((1, D_RNN)),
                  _full((1, D_RNN)), _full((D_RNN, D_MODEL))],
        out_specs=pl.BlockSpec((1, tt, D_MODEL), lambda i, j: (i, j, 0)),
        out_shape=jax.ShapeDtypeStruct((b, s, D_MODEL), F32),
        scratch_shapes=[pltpu.VMEM((tt + HALO, D_RNN), F32), pltpu.VMEM((HALO, D_RNN), F32)],
        compiler_params=_cparams("arbitrary", "arbitrary"),
        name="rglru",
    )(rx3, rg3, conv_w, row(conv_b), _block_diag(w_a).astype(BF16), row(b_a),
      _block_diag(w_x).astype(BF16), row(b_x), row(lam), w_out.astype(BF16))


def _compress_kernel(rk_ref, rv_ref, pek_ref, pev_ref, w1k_ref, w1v_ref, w2k_ref, w2v_ref, gk_ref,
                     kc_ref, vc_ref):
    half = CMP_STRIDE * HEAD_DIM
    ncp = rk_ref.shape[2]

    def mlp(r_ref, pe_ref, w1_ref, w2_ref):
        r = r_ref[0, 0]
        pe = pe_ref[...]
        lo = _dot((r + pe[:, :half]).astype(BF16), w1_ref[:half, :])
        hi = _dot((r + pe[:, half:]).astype(BF16), w1_ref[half:, :])
        hid = lo + pltpu.roll(hi, ncp - 1, 0)
        return _dot(_gelu(hid).astype(BF16), w2_ref[...])

    kc_ref[0, 0] = _rms(mlp(rk_ref, pek_ref, w1k_ref, w2k_ref), gk_ref[...])
    vc_ref[0, 0] = mlp(rv_ref, pev_ref, w1v_ref, w2v_ref)


def _compress(rk, rv, pe_k, pe_v, w1k, w1v, w2k, w2v, g_cmp):
    b, g, ncp, wide = rk.shape
    blk = pl.BlockSpec((1, 1, ncp, wide), lambda i, j: (i, j, 0, 0))
    out = pl.BlockSpec((1, 1, ncp, HEAD_DIM), lambda i, j: (i, j, 0, 0))
    w1 = _full((CMP_LEN * HEAD_DIM, CMP_HIDDEN))
    w2 = _full((CMP_HIDDEN, HEAD_DIM))
    pe = _full((1, CMP_LEN * HEAD_DIM))
    return pl.pallas_call(
        _compress_kernel,
        grid=(b, g),
        in_specs=[blk, blk, pe, pe, w1, w1, w2, w2, _full((1, HEAD_DIM))],
        out_specs=[out, out],
        out_shape=[jax.ShapeDtypeStruct((b, g, ncp, HEAD_DIM), F32)] * 2,
        compiler_params=_cparams("parallel", "parallel"),
        name="compress",
    )(rk, rv, pe_k.reshape(1, -1), pe_v.reshape(1, -1), w1k.astype(BF16), w1v.astype(BF16),
      w2k.astype(BF16), w2v.astype(BF16), g_cmp.reshape(1, HEAD_DIM))


def _exp_terms(sm, static):
    if static:
        return jnp.exp2(sm)
    return jnp.exp2(sm - jnp.max(sm, axis=0, keepdims=True))


def _normalise(acc):
    return acc[:HEAD_DIM] * (1.0 / acc[HEAD_DIM:])


def _attn_kernel(flag_ref, q_ref, gt_ref, kc_ref, vct_ref, ovt_ref, ka_ref, vst_ref, kw_ref, vwt_ref, wb_ref,
                   o_ref):
    qi = pl.program_id(2)
    st = qi * TQ
    cols = GROUP * TQ
    ncp = kc_ref.shape[2]
    static_ok = flag_ref[0, 0] != 0
    both = lambda f: lax.cond(static_ok, lambda: f(True), lambda: f(False))

    qt = q_ref[0, 0, 0]
    q_loc = lax.broadcasted_iota(jnp.int32, (1, cols), 1) & (TQ - 1)
    t_col = st + q_loc

    def compressed(static):
        def over(n):
            cidx = lax.broadcasted_iota(jnp.int32, (n, 1), 0)
            valid = (cidx * CMP_STRIDE + (CMP_LEN - 1)) <= t_col
            sm = jnp.where(valid, _dot(kc_ref[0, 0, 0:n, :], qt), NEG)
            e = _exp_terms(sm, True) if static else jnp.where(valid, _exp_terms(sm, False), 0.0)
            l = jnp.sum(e, axis=0, keepdims=True)
            p = e * (1.0 / jnp.where(l > 0.0, l, 1.0))
            psum = p[:, 0:TQ] + p[:, TQ:2 * TQ] + p[:, 2 * TQ:3 * TQ] + p[:, 3 * TQ:4 * TQ]
            if n < ncp:
                psum = jnp.concatenate([psum, jnp.zeros((ncp - n, TQ), F32)], axis=0)
            return _dot(vct_ref[0, 0, :, 0:n], p.astype(BF16)), psum

        half = ncp // 2
        if half % 128:
            return over(ncp)
        return lax.cond(st + TQ <= half * CMP_STRIDE + CMP_LEN - 1, lambda: over(half), lambda: over(ncp))

    o_cmp, psum = both(compressed)

    p_hi, p_lo = _split_bf16(psum)
    imp_t = _dot(ovt_ref[...], p_hi) + _dot(ovt_ref[...], p_lo)
    cur = lax.shift_right_logical(st + lax.broadcasted_iota(jnp.int32, (1, TQ), 1), SEL_LEN.bit_length() - 1)

    def select(nb):
        jb = lax.broadcasted_iota(jnp.int32, (nb, 1), 0)
        jbf = jb.astype(F32)
        forced = (jb == 0) | (jb == cur) | (jb == cur - 1)
        sel = forced
        work = jnp.where((jb <= cur) & jnp.logical_not(forced), imp_t[:nb], -FORCE)
        for _ in range(N_SELECT - N_FORCED):
            mx = jnp.max(work, axis=0, keepdims=True)
            first = jnp.min(jnp.where(work == mx, jbf, float(SEL_LANES)), axis=0, keepdims=True)
            pick = jbf == first
            sel = sel | pick
            work = jnp.where(pick, -jnp.inf, work)
        bias = jnp.where(sel & (jb <= cur), 0.0, NEG)
        if nb < SEL_LANES:
            bias = jnp.concatenate([bias, jnp.full((SEL_LANES - nb, TQ), NEG, F32)], axis=0)
        return bias.astype(BF16)

    n_full = lax.shift_right_logical(st, SEL_TRIP.bit_length() - 1)
    causal = jnp.where(lax.broadcasted_iota(jnp.int32, (TQ, 1), 0) <= q_loc, 0.0, NEG)

    def selected(static):
        def scores(keys):
            return _dot(ka_ref[0, 0, keys, :], qaug), vst_ref[0, 0, :, keys]

        def last(before):
            s_t, v_t = scores(pl.ds(pl.multiple_of(st - before, TQ), before + TQ))
            own = s_t[before:] + causal
            s_t = jnp.concatenate([s_t[:before], own], axis=0) if before else own
            if static:
                return _dot(v_t, _exp_terms(s_t, True).astype(BF16))
            m = jnp.max(s_t, axis=0, keepdims=True)
            return m, _dot(v_t, jnp.exp2(s_t - m).astype(BF16))

        rem = lax.shift_right_logical(st & (SEL_TRIP - 1), TQ.bit_length() - 1)
        first = lax.switch(rem, [functools.partial(last, i * TQ) for i in range(SEL_TRIP // TQ)])
        trip_keys = lambda c: pl.ds(pl.multiple_of(c * SEL_TRIP, SEL_TRIP), SEL_TRIP)
        if static:
            def trip(c, acc):
                s_t, v_t = scores(trip_keys(c))
                return acc + _dot(v_t, jnp.exp2(s_t).astype(BF16))

            return lax.fori_loop(0, n_full, trip, first)

        def trip(c, carry):
            m, acc = carry
            s_t, v_t = scores(trip_keys(c))
            mn = jnp.maximum(m, jnp.max(s_t, axis=0, keepdims=True))
            return mn, jnp.exp2(m - mn) * acc + _dot(v_t, jnp.exp2(s_t - mn).astype(BF16))

        return lax.fori_loop(0, n_full, trip, first)[1]

    def window(static):
        wkeys = pl.ds(pl.multiple_of(st, TQ), WINDOW + TQ)
        s_t = _dot(kw_ref[0, 0, wkeys, :], qt)
        e = jnp.concatenate([_exp_terms(s_t[:, r * TQ:(r + 1) * TQ] + wb_ref[0], static)
                             for r in range(GROUP)], axis=1)
        return _dot(vwt_ref[0, 0, :, wkeys], e.astype(BF16))

    tiles_per_arm = SEL_ARM_ROWS * SEL_LEN // TQ
    n_arms = SEL_LANES // SEL_ARM_ROWS
    arm = lambda nb, static: (select(nb), window(static))
    bias, acc_w = both(lambda static: lax.switch(
        jnp.minimum(lax.div(qi, tiles_per_arm), n_arms - 1),
        [functools.partial(arm, SEL_ARM_ROWS * (i + 1), static) for i in range(n_arms)]))
    qaug = jnp.concatenate([jnp.concatenate([bias] * GROUP, axis=1), qt], axis=0)
    o_slc = _normalise(both(selected))
    o_win = _normalise(acc_w)

    sg = _sigmoid(gt_ref[0]).T
    heads = []
    for r in range(GROUP):
        acc = jnp.zeros((HEAD_DIM, TQ), F32)
        for j, y in enumerate((o_cmp, o_slc, o_win)):
            acc = acc + sg[3 * r + j:3 * r + j + 1, :] * y[:, r * TQ:(r + 1) * TQ]
        heads.append(acc)
    o_ref[0] = jnp.concatenate(heads, axis=0).T


def _attention(flag, q_t, gates3, kcmp, vcmp_t, overlap_t, kaug, vslc_t, kwin, vwin_t):
    b, _, n_tiles, _, cols = q_t.shape
    s = n_tiles * TQ
    ncp = kcmp.shape[2]
    sp = kwin.shape[2]
    per_bg = lambda shape: pl.BlockSpec((1, 1) + shape, lambda i, j, k: (i, j, 0, 0))
    return pl.pallas_call(
        _attn_kernel,
        grid=(b, N_KV, n_tiles),
        in_specs=[pl.BlockSpec(memory_space=pltpu.SMEM),
                  pl.BlockSpec((1, 1, 1, QK_LANES, cols), lambda i, j, k: (i, j, k, 0, 0)),
                  pl.BlockSpec((1, TQ, GATE_LANES), lambda i, j, k: (i, k, j)),
                  per_bg((ncp, QK_LANES)), per_bg((HEAD_DIM, ncp)), _full((SEL_LANES, ncp)),
                  per_bg((s, SEL_LANES + QK_LANES)), per_bg((V_LANES, s)),
                  per_bg((sp, QK_LANES)), per_bg((V_LANES, sp)),
                  pl.BlockSpec((1, WINDOW + TQ, TQ), lambda i, j, k: (jnp.minimum(k, WINDOW // TQ), 0, 0))],
        out_specs=pl.BlockSpec((1, TQ, KV_W), lambda i, j, k: (i, k, j)),
        out_shape=jax.ShapeDtypeStruct((b, s, N_HEADS * HEAD_DIM), F32),
        compiler_params=_cparams("parallel", "parallel", "arbitrary"),
        name="nsa_attention",
    )(flag, q_t, gates3, kcmp, vcmp_t, overlap_t, kaug, vslc_t, kwin, vwin_t, _band_biases())


def _band_biases():
    q = np.arange(TQ)[None, :]
    kw = np.arange(WINDOW + TQ)[:, None] - WINDOW
    in_band = (kw <= q) & (kw > q - WINDOW)
    bands = [np.where(in_band & (kw + tile * TQ >= 0), 0.0, NEG) for tile in range(WINDOW // TQ + 1)]
    return jnp.asarray(np.stack(bands), F32)


def _overlap_matrix_t(s):
    ncp = s // CMP_STRIDE
    i = np.arange(ncp)[None, :]
    j = np.arange(SEL_LANES)[:, None]
    lo = np.maximum(i * CMP_STRIDE, j * SEL_LEN)
    hi = np.minimum(i * CMP_STRIDE + CMP_LEN, (j + 1) * SEL_LEN)
    ov = np.maximum(hi - lo, 0) / CMP_STRIDE
    ov = ov * (i < (s - CMP_LEN) // CMP_STRIDE + 1) * (j < s // SEL_LEN)
    return jnp.asarray(ov, dtype=BF16)


def _static_flag(q_g, *k_gs):
    kmax = functools.reduce(jnp.maximum, [jnp.max(jnp.abs(g)) for g in k_gs])
    bound = HEAD_DIM * SCALE * jnp.max(jnp.abs(q_g)) * kmax
    return (bound < STATIC_SCORE_BOUND).astype(jnp.int32).reshape(1, 1)


def _merge_kernel(x_ref, yr_ref, o_ref, mg_ref, wa_ref, wo_ref, g2_ref, x1_ref, hn_ref):
    y_attn = _dot(o_ref[...].astype(BF16), wa_ref[...])
    mg = mg_ref[...]
    m = _sigmoid(mg[:, :D_MODEL]) * yr_ref[...] + _sigmoid(mg[:, D_MODEL:]) * y_attn
    x1 = x_ref[...] + _dot(m.astype(BF16), wo_ref[...])
    x1_ref[...] = x1
    hn_ref[...] = _rms(x1, g2_ref[...]).astype(BF16)


def _merge(x2, y_rnn, o_attn, mg, w_attn_out, w_out, norm2_g):
    t = x2.shape[0]
    tm = _tile(t, 512)
    rowblk = lambda n: pl.BlockSpec((tm, n), lambda i: (i, 0))
    return pl.pallas_call(
        _merge_kernel,
        grid=(t // tm,),
        in_specs=[rowblk(D_MODEL), rowblk(D_MODEL), rowblk(D_MODEL), rowblk(2 * D_MODEL),
                  _full((D_MODEL, D_MODEL)), _full((D_MODEL, D_MODEL)), _full((1, D_MODEL))],
        out_specs=[rowblk(D_MODEL), rowblk(D_MODEL)],
        out_shape=[jax.ShapeDtypeStruct((t, D_MODEL), F32), jax.ShapeDtypeStruct((t, D_MODEL), BF16)],
        compiler_params=_cparams("parallel"),
        name="merge",
    )(x2, y_rnn, o_attn, mg, w_attn_out.astype(BF16), w_out.astype(BF16), norm2_g.reshape(1, D_MODEL))


def _ffn_up_kernel(hp_ref, h_ref, wg_ref, wv_ref, cwg_ref, cwv_ref, cbg_ref, cbv_ref, act_ref, up_ref,
                   *, tm, tiles_per_seq):
    first = (pl.program_id(0) % tiles_per_seq) == 0
    keep = jnp.where(first, 0.0, 1.0)

    def conv(w_ref, cw_ref, cb_ref):
        up_ref[0:HALO, :] = _dot(hp_ref[...], w_ref[...]) * keep
        up_ref[HALO:HALO + tm, :] = _dot(h_ref[...], w_ref[...])
        cw = cw_ref[...]
        y = cb_ref[...] + cw[FFN_CONV_W - 1:FFN_CONV_W, :] * up_ref[HALO:HALO + tm, :]
        for k in range(FFN_CONV_W - 1):
            off = HALO - (FFN_CONV_W - 1) + k
            y = y + cw[k:k + 1, :] * up_ref[off:off + tm, :]
        return y

    gate = conv(wg_ref, cwg_ref, cbg_ref)
    val = conv(wv_ref, cwv_ref, cbv_ref)
    act_ref[...] = (_gelu(gate) * val).astype(BF16)


def _ffn_up(hn, w_up, conv_w, conv_b, s):
    t = hn.shape[0]
    tm = _tile(s, 1024)
    tn = 512
    nj = D_FF // tn
    wb = w_up.astype(BF16)
    cb = conv_b.reshape(1, 2 * D_FF)
    return pl.pallas_call(
        functools.partial(_ffn_up_kernel, tm=tm, tiles_per_seq=s // tm),
        grid=(t // tm, nj),
        in_specs=[pl.BlockSpec((HALO, D_MODEL), lambda i, j: (jnp.maximum(i * (tm // HALO) - 1, 0), 0)),
                  pl.BlockSpec((tm, D_MODEL), lambda i, j: (i, 0)),
                  pl.BlockSpec((D_MODEL, tn), lambda i, j: (0, j)),
                  pl.BlockSpec((D_MODEL, tn), lambda i, j: (0, j + nj)),
                  pl.BlockSpec((FFN_CONV_W, tn), lambda i, j: (0, j)),
                  pl.BlockSpec((FFN_CONV_W, tn), lambda i, j: (0, j + nj)),
                  pl.BlockSpec((1, tn), lambda i, j: (0, j)),
                  pl.BlockSpec((1, tn), lambda i, j: (0, j + nj))],
        out_specs=pl.BlockSpec((tm, tn), lambda i, j: (i, j)),
        out_shape=jax.ShapeDtypeStruct((t, D_FF), BF16),
        scratch_shapes=[pltpu.VMEM((tm + HALO, tn), F32)],
        compiler_params=_cparams("parallel", "arbitrary"),
        name="ffn_up",
    )(hn, hn, wb, wb, conv_w, conv_w, cb, cb)


def _ffn_down_kernel(x1_ref, act_ref, w_ref, o_ref):
    o_ref[...] = x1_ref[...] + _dot(act_ref[...], w_ref[...])


def _ffn_down(x1, act, w_down):
    t = x1.shape[0]
    tm = _tile(t, 512)
    return pl.pallas_call(
        _ffn_down_kernel,
        grid=(t // tm,),
        in_specs=[pl.BlockSpec((tm, D_MODEL), lambda i: (i, 0)), pl.BlockSpec((tm, D_FF), lambda i: (i, 0)),
                  _full((D_FF, D_MODEL))],
        out_specs=pl.BlockSpec((tm, D_MODEL), lambda i: (i, 0)),
        out_shape=jax.ShapeDtypeStruct((t, D_MODEL), F32),
        compiler_params=_cparams("parallel"),
        name="ffn_down",
    )(x1, act, w_down.astype(BF16))


def _cmp_rows(a, b, s):
    a = a.reshape(b, s // CMP_STRIDE, CMP_STRIDE, N_KV, HEAD_DIM).transpose(0, 3, 1, 2, 4)
    return a.reshape(b, N_KV, s // CMP_STRIDE, CMP_STRIDE * HEAD_DIM)


def kernel(x, norm1_g, w_in, rnn_conv_w, rnn_conv_b, rg_w_a, rg_b_a, rg_w_x, rg_b_x, lru_lambda, w_rnn_out, q_norm_g, k_cmp_norm_g, k_slc_norm_g, k_win_norm_g, cmp_pe_k, cmp_w1_k, cmp_w2_k, cmp_pe_v, cmp_w1_v, cmp_w2_v, w_attn_out, w_out, norm2_g, w_up, ffn_conv_w, ffn_conv_b, w_down):
    b, s, d = x.shape
    assert d == D_MODEL and s % SEL_TRIP == 0 and s // SEL_LEN <= SEL_LANES
    t = b * s
    x2 = x.reshape(t, d)

    rx, rgate, kvc, gates, mg, qpad, kaug_t, vslc, kwin_t, vwin = _in_proj(
        x2, b, s, norm1_g, w_in, q_norm_g, k_slc_norm_g, k_win_norm_g)

    y_rnn = _rglru(rx.reshape(b, s, D_RNN), rgate.reshape(b, s, D_RNN), rnn_conv_w, rnn_conv_b,
                   rg_w_a, rg_b_a, rg_w_x, rg_b_x, lru_lambda, w_rnn_out).reshape(t, d)

    kcmp, vcmp = _compress(_cmp_rows(kvc[:, :KV_W], b, s), _cmp_rows(kvc[:, KV_W:], b, s),
                           cmp_pe_k, cmp_pe_v, cmp_w1_k, cmp_w1_v, cmp_w2_k, cmp_w2_v, k_cmp_norm_g)
    kcmp_pad = jnp.pad(kcmp.astype(BF16), ((0, 0), (0, 0), (0, 0), (0, QK_LANES - HEAD_DIM)))
    vcmp_t = vcmp.transpose(0, 1, 3, 2).astype(BF16)
    flag = _static_flag(q_norm_g, k_cmp_norm_g, k_slc_norm_g, k_win_norm_g)
    o_attn = _attention(flag, qpad, gates.reshape(b, s, N_KV * GATE_LANES), kcmp_pad, vcmp_t,
                          _overlap_matrix_t(s), kaug_t, vslc, kwin_t, vwin).reshape(t, d)

    x1, hn = _merge(x2, y_rnn, o_attn, mg, w_attn_out, w_out, norm2_g)
    act = _ffn_up(hn, w_up, ffn_conv_w, ffn_conv_b, s)
    return _ffn_down(x1, act, w_down).reshape(b, s, d)
```

```python
import functools

import numpy as np
import jax
import jax.numpy as jnp
from jax import lax
from jax.experimental import pallas as pl
from jax.experimental.pallas import tpu as pltpu

F32 = jnp.float32
BF16 = jnp.bfloat16

D_MODEL = 1024
D_RNN = 1344
RNN_BLOCKS = 16
RNN_CONV_W = 4
LRU_C = 8.0
N_HEADS = 16
N_KV = 4
HEAD_DIM = 64
GROUP = N_HEADS // N_KV
KV_W = N_KV * HEAD_DIM
CMP_LEN = 32
CMP_STRIDE = 16
CMP_HIDDEN = 128
SEL_LEN = 64
N_SELECT = 16
N_FORCED = 3
WINDOW = 512
D_FF = 3 * D_MODEL
FFN_CONV_W = 3
EPS = 1e-6
NEG = -1e30
FORCE = 1e6
SCALE = HEAD_DIM ** -0.5

SEL_LANES = 128
V_LANES = 128
QK_LANES = 128
GATE_LANES = 128
TQ = 256
SEL_TRIP = 1024
SEL_ARM_ROWS = 32
LOG2E = float(np.log2(np.e))
HALO = 8
VMEM_LIMIT = 56 * 1024 * 1024
STATIC_SCORE_BOUND = 40.0


def _cparams(*sem):
    return pltpu.CompilerParams(dimension_semantics=sem, vmem_limit_bytes=VMEM_LIMIT)


def _gelu(x):
    c = float(np.sqrt(2.0 / np.pi))
    half = 0.5 * x
    return half + half * jnp.tanh(x * (c + (c * 0.044715) * (x * x)))


def _sigmoid(x):
    return 0.5 + 0.5 * jnp.tanh(0.5 * x)


def _rms(x, g):
    ms = jnp.mean(x * x, axis=-1, keepdims=True)
    return x * lax.rsqrt(ms + EPS) * g


def _dot(a, b):
    return jnp.dot(a, b, preferred_element_type=F32)


def _dot_nt(a, b):
    return lax.dot_general(a, b, (((1,), (1,)), ((), ())), preferred_element_type=F32)


def _split_bf16(x):
    hi = x.astype(BF16)
    lo = (x - hi.astype(F32)).astype(BF16)
    return hi, lo


def _tile(n, want):
    t = min(n, want)
    while n % t:
        t //= 2
    return t


def _full(shape):
    nd = len(shape)
    return pl.BlockSpec(shape, lambda *_: (0,) * nd)


def _in_proj_kernel(x_ref, g_ref, w_rx, w_rg, w_q, w_kv, w_g, w_mg, qg_ref, gs_ref, gw_ref, eye_ref,
                    kw0_ref, vw0_ref,
                    o_rx, o_rg, o_cmp, o_g, o_mg, q_ref, ka_ref, vs_ref, kw_ref, vw_ref, *, tiles_per_seq):
    del kw0_ref, vw0_ref
    h = _rms(x_ref[...], g_ref[...]).astype(BF16)

    q = _dot(h, w_q[...])
    kv = _dot(h, w_kv[...])
    o_rx[...] = _dot(h, w_rx[...])
    for g in range(N_KV):
        for r in range(GROUP):
            lo = (g * GROUP + r) * HEAD_DIM
            y = _rms(q[:, lo:lo + HEAD_DIM], qg_ref[...]) * (SCALE * LOG2E)
            q_ref[0, g, 0, :, r * TQ:(r + 1) * TQ] = _dot_nt(eye_ref[...], y.astype(BF16)).astype(BF16)

    o_rg[...] = _dot(h, w_rg[...])
    o_mg[...] = _dot(h, w_mg[...])
    o_cmp[...] = kv[:, :2 * KV_W]
    pos = (pl.program_id(0) % tiles_per_seq) * TQ + lax.broadcasted_iota(jnp.int32, (TQ, 1), 0)
    blk = lax.shift_right_logical(pos, SEL_LEN.bit_length() - 1)
    onehot = (lax.broadcasted_iota(jnp.int32, (1, SEL_LANES), 1) == blk).astype(BF16)
    zeros = jnp.zeros((TQ, QK_LANES - HEAD_DIM), BF16)
    ones = jnp.ones((V_LANES - HEAD_DIM, TQ), BF16)
    for g in range(N_KV):
        lo = g * HEAD_DIM
        ka_ref[0, g, :, 0:SEL_LANES] = onehot
        ka_ref[0, g, :, SEL_LANES:SEL_LANES + HEAD_DIM] = _rms(
            kv[:, 2 * KV_W + lo:2 * KV_W + lo + HEAD_DIM], gs_ref[...]).astype(BF16)
        ka_ref[0, g, :, SEL_LANES + HEAD_DIM:SEL_LANES + QK_LANES] = zeros
        kw_ref[0, g, :, 0:HEAD_DIM] = _rms(kv[:, 4 * KV_W + lo:4 * KV_W + lo + HEAD_DIM], gw_ref[...]).astype(BF16)
        kw_ref[0, g, :, HEAD_DIM:QK_LANES] = zeros
        for src, dst in ((3, vs_ref), (5, vw_ref)):
            v = kv[:, src * KV_W + lo:src * KV_W + lo + HEAD_DIM].astype(BF16)
            dst[0, g, 0:HEAD_DIM, :] = _dot_nt(eye_ref[0:HEAD_DIM, :], v).astype(BF16)
            dst[0, g, HEAD_DIM:V_LANES, :] = ones

    o_g[...] = _dot(h, w_g[...])


def _in_proj(x2, b, s, norm1_g, w_in, q_g, g_slc, g_win):
    t = x2.shape[0]
    nq = s // TQ
    pad_blocks = WINDOW // TQ
    kvw = 6 * KV_W
    o0 = 2 * D_RNN
    o1 = o0 + N_HEADS * HEAD_DIM
    o2 = o1 + kvw
    o3 = o2 + 3 * N_HEADS
    wb = w_in.astype(BF16)
    w_g = jnp.pad(wb[:, o2:o3].reshape(D_MODEL, N_KV, 3 * GROUP), ((0, 0), (0, 0), (0, GATE_LANES - 3 * GROUP)))
    w_g = w_g.reshape(D_MODEL, N_KV * GATE_LANES)
    ws = [wb[:, :D_RNN], wb[:, D_RNN:o0], wb[:, o0:o1], wb[:, o1:o2], w_g, wb[:, o3:]]
    eye = jnp.asarray(np.eye(QK_LANES, HEAD_DIM), BF16)
    gain = lambda v: v.reshape(1, HEAD_DIM)
    rows = lambda n: pl.BlockSpec((TQ, n), lambda i: (i, 0))
    flat = lambda n: jax.ShapeDtypeStruct((t, n), F32)
    kwin0 = jnp.zeros((b, N_KV, s + WINDOW, QK_LANES), BF16)
    vwin0 = jnp.zeros((b, N_KV, V_LANES, s + WINDOW), BF16)
    n_in = 14
    return pl.pallas_call(
        functools.partial(_in_proj_kernel, tiles_per_seq=nq),
        grid=(t // TQ,),
        in_specs=[rows(D_MODEL), _full((1, D_MODEL))] + [_full(w.shape) for w in ws]
        + [_full((1, HEAD_DIM))] * 3 + [_full((QK_LANES, HEAD_DIM))]
        + [pl.BlockSpec(memory_space=pl.ANY)] * 2,
        out_specs=[rows(D_RNN), rows(D_RNN), rows(2 * KV_W), rows(N_KV * GATE_LANES), rows(2 * D_MODEL),
                   pl.BlockSpec((1, N_KV, 1, QK_LANES, GROUP * TQ), lambda i: (i // nq, 0, i % nq, 0, 0)),
                   pl.BlockSpec((1, N_KV, TQ, SEL_LANES + QK_LANES), lambda i: (i // nq, 0, i % nq, 0)),
                   pl.BlockSpec((1, N_KV, V_LANES, TQ), lambda i: (i // nq, 0, 0, i % nq)),
                   pl.BlockSpec((1, N_KV, TQ, QK_LANES), lambda i: (i // nq, 0, i % nq + pad_blocks, 0)),
                   pl.BlockSpec((1, N_KV, V_LANES, TQ), lambda i: (i // nq, 0, 0, i % nq + pad_blocks))],
        out_shape=[flat(D_RNN), flat(D_RNN), flat(2 * KV_W), flat(N_KV * GATE_LANES), flat(2 * D_MODEL),
                   jax.ShapeDtypeStruct((b, N_KV, nq, QK_LANES, GROUP * TQ), BF16),
                   jax.ShapeDtypeStruct((b, N_KV, s, SEL_LANES + QK_LANES), BF16),
                   jax.ShapeDtypeStruct((b, N_KV, V_LANES, s), BF16),
                   jax.ShapeDtypeStruct(kwin0.shape, BF16), jax.ShapeDtypeStruct(vwin0.shape, BF16)],
        input_output_aliases={n_in - 2: 8, n_in - 1: 9},
        compiler_params=_cparams("parallel"),
        name="in_proj",
    )(x2, norm1_g.reshape(1, D_MODEL), *ws, gain(q_g), gain(g_slc), gain(g_win), eye, kwin0, vwin0)


def _rglru_kernel(rx_ref, rg_ref, cw_ref, cb_ref, wa_ref, ba_ref, wx_ref, bx_ref, lam_ref, wo_ref,
                  y_ref, xs_ref, carry_ref, *, tt):
    ti = pl.program_id(1)

    @pl.when(ti == 0)
    def _():
        xs_ref[0:HALO, :] = jnp.zeros((HALO, D_RNN), F32)
        carry_ref[...] = jnp.zeros((HALO, D_RNN), F32)

    xs_ref[HALO:HALO + tt, :] = rx_ref[0]
    cw = cw_ref[...]
    x = cb_ref[...] + cw[RNN_CONV_W - 1:RNN_CONV_W, :] * xs_ref[HALO:HALO + tt, :]
    for k in range(RNN_CONV_W - 1):
        off = HALO - (RNN_CONV_W - 1) + k
        x = x + cw[k:k + 1, :] * xs_ref[off:off + tt, :]
    xs_ref[0:HALO, :] = xs_ref[tt:tt + HALO, :]

    xb = x.astype(BF16)
    r = _sigmoid(_dot(xb, wa_ref[...]) + ba_ref[...])
    gi = _sigmoid(_dot(xb, wx_ref[...]) + bx_ref[...])
    lam = lam_ref[...]
    log_sig = jnp.minimum(lam, 0.0) - jnp.log1p(jnp.exp(-jnp.abs(lam)))
    log_a = LRU_C * r * log_sig
    a = jnp.exp(log_a)
    y = 1.0 - a * a
    u = jnp.where(y > 0.0, y * lax.rsqrt(y), 0.0) * (gi * x)

    a = a.reshape(tt // HALO, HALO, D_RNN)
    u = u.reshape(tt // HALO, HALO, D_RNN)
    sub = lax.broadcasted_iota(jnp.int32, (1, HALO, 1), 1)
    for d in (1, 2, 4):
        keep = sub >= d
        a_sh = pltpu.roll(a, d, 1)
        u_sh = pltpu.roll(u, d, 1)
        u = jnp.where(keep, a * u_sh + u, u)
        a = jnp.where(keep, a * a_sh, a)
    carry = carry_ref[...]
    hs = []
    for g in range(tt // HALO):
        h = a[g] * carry + u[g]
        hs.append(h)
        carry = jnp.broadcast_to(h[HALO - 1:HALO, :], (HALO, D_RNN))
    carry_ref[...] = carry

    hg = (jnp.concatenate(hs, axis=0) * _gelu(rg_ref[0])).astype(BF16)
    y_ref[0] = _dot(hg, wo_ref[...])


def _block_diag(w):
    n, c, d = w.shape
    return jnp.einsum("ncd,nm->ncmd", w, jnp.eye(n, dtype=w.dtype)).reshape(n * c, n * d)


def _rglru(rx3, rg3, conv_w, conv_b, w_a, b_a, w_x, b_x, lam, w_out):
    b, s, _ = rx3.shape
    tt = _tile(s, 256)
    row = lambda v: v.reshape(1, D_RNN)
    return pl.pallas_call(
        functools.partial(_rglru_kernel, tt=tt),
        grid=(b, s // tt),
        in_specs=[pl.BlockSpec((1, tt, D_RNN), lambda i, j: (i, j, 0)),
                  pl.BlockSpec((1, tt, D_RNN), lambda i, j: (i, j, 0)),
                  _full((RNN_CONV_W, D_RNN)), _full((1, D_RNN)),
                  _full((D_RNN, D_RNN)), _full((1, D_RNN)),
                  _full((D_RNN, D_RNN)), _full((1, D_RNN)),
                  _full((1, D_RNN)), _full((D_RNN, D_MODEL))],
        out_specs=pl.BlockSpec((1, tt, D_MODEL), lambda i, j: (i, j, 0)),
        out_shape=jax.ShapeDtypeStruct((b, s, D_MODEL), F32),
        scratch_shapes=[pltpu.VMEM((tt + HALO, D_RNN), F32), pltpu.VMEM((HALO, D_RNN), F32)],
        compiler_params=_cparams("arbitrary", "arbitrary"),
        name="rglru",
    )(rx3, rg3, conv_w, row(conv_b), _block_diag(w_a).astype(BF16), row(b_a),
      _block_diag(w_x).astype(BF16), row(b_x), row(lam), w_out.astype(BF16))


def _compress_kernel(rk_ref, rv_ref, pek_ref, pev_ref, w1k_ref, w1v_ref, w2k_ref, w2v_ref, gk_ref,
                     kc_ref, vc_ref):
    half = CMP_STRIDE * HEAD_DIM
    ncp = rk_ref.shape[2]

    def mlp(r_ref, pe_ref, w1_ref, w2_ref):
        r = r_ref[0, 0]
        pe = pe_ref[...]
        lo = _dot((r + pe[:, :half]).astype(BF16), w1_ref[:half, :])
        hi = _dot((r + pe[:, half:]).astype(BF16), w1_ref[half:, :])
        hid = lo + pltpu.roll(hi, ncp - 1, 0)
        return _dot(_gelu(hid).astype(BF16), w2_ref[...])

    kc_ref[0, 0] = _rms(mlp(rk_ref, pek_ref, w1k_ref, w2k_ref), gk_ref[...])
    vc_ref[0, 0] = mlp(rv_ref, pev_ref, w1v_ref, w2v_ref)


def _compress(rk, rv, pe_k, pe_v, w1k, w1v, w2k, w2v, g_cmp):
    b, g, ncp, wide = rk.shape
    blk = pl.BlockSpec((1, 1, ncp, wide), lambda i, j: (i, j, 0, 0))
    out = pl.BlockSpec((1, 1, ncp, HEAD_DIM), lambda i, j: (i, j, 0, 0))
    w1 = _full((CMP_LEN * HEAD_DIM, CMP_HIDDEN))
    w2 = _full((CMP_HIDDEN, HEAD_DIM))
    pe = _full((1, CMP_LEN * HEAD_DIM))
    return pl.pallas_call(
        _compress_kernel,
        grid=(b, g),
        in_specs=[blk, blk, pe, pe, w1, w1, w2, w2, _full((1, HEAD_DIM))],
        out_specs=[out, out],
        out_shape=[jax.ShapeDtypeStruct((b, g, ncp, HEAD_DIM), F32)] * 2,
        compiler_params=_cparams("parallel", "parallel"),
        name="compress",
    )(rk, rv, pe_k.reshape(1, -1), pe_v.reshape(1, -1), w1k.astype(BF16), w1v.astype(BF16),
      w2k.astype(BF16), w2v.astype(BF16), g_cmp.reshape(1, HEAD_DIM))


def _exp_terms(sm, static):
    if static:
        return jnp.exp2(sm)
    return jnp.exp2(sm - jnp.max(sm, axis=0, keepdims=True))


def _normalise(acc):
    return acc[:HEAD_DIM] * (1.0 / acc[HEAD_DIM:])


def _attn_kernel(q_ref, gt_ref, kc_ref, vct_ref, ovt_ref, ka_ref, vst_ref, kw_ref, vwt_ref, wb_ref, o_ref,
                 *, static):
    qi = pl.program_id(2)
    st = qi * TQ
    cols = GROUP * TQ
    ncp = kc_ref.shape[2]
    both = lambda f: f(static)

    qt = q_ref[0, 0, 0]
    q_loc = lax.broadcasted_iota(jnp.int32, (1, cols), 1) & (TQ - 1)
    t_col = st + q_loc

    def compressed(static):
        def over(n):
            cidx = lax.broadcasted_iota(jnp.int32, (n, 1), 0)
            valid = (cidx * CMP_STRIDE + (CMP_LEN - 1)) <= t_col
            sm = jnp.where(valid, _dot(kc_ref[0, 0, 0:n, :], qt), NEG)
            e = _exp_terms(sm, True) if static else jnp.where(valid, _exp_terms(sm, False), 0.0)
            l = jnp.sum(e, axis=0, keepdims=True)
            p = e * (1.0 / jnp.where(l > 0.0, l, 1.0))
            psum = p[:, 0:TQ] + p[:, TQ:2 * TQ] + p[:, 2 * TQ:3 * TQ] + p[:, 3 * TQ:4 * TQ]
            if n < ncp:
                psum = jnp.concatenate([psum, jnp.zeros((ncp - n, TQ), F32)], axis=0)
            return _dot(vct_ref[0, 0, :, 0:n], p.astype(BF16)), psum

        half = ncp // 2
        if half % 128:
            return over(ncp)
        return lax.cond(st + TQ <= half * CMP_STRIDE + CMP_LEN - 1, lambda: over(half), lambda: over(ncp))

    o_cmp, psum = both(compressed)

    p_hi, p_lo = _split_bf16(psum)
    imp_t = _dot(ovt_ref[...], p_hi) + _dot(ovt_ref[...], p_lo)
    cur = lax.shift_right_logical(st + lax.broadcasted_iota(jnp.int32, (1, TQ), 1), SEL_LEN.bit_length() - 1)

    def select(nb):
        jb = lax.broadcasted_iota(jnp.int32, (nb, 1), 0)
        jbf = jb.astype(F32)
        forced = (jb == 0) | (jb == cur) | (jb == cur - 1)
        sel = forced
        work = jnp.where((jb <= cur) & jnp.logical_not(forced), imp_t[:nb], -FORCE)
        for _ in range(N_SELECT - N_FORCED):
            mx = jnp.max(work, axis=0, keepdims=True)
            first = jnp.min(jnp.where(work == mx, jbf, float(SEL_LANES)), axis=0, keepdims=True)
            pick = jbf == first
            sel = sel | pick
            work = jnp.where(pick, -jnp.inf, work)
        bias = jnp.where(sel & (jb <= cur), 0.0, NEG)
        if nb < SEL_LANES:
            bias = jnp.concatenate([bias, jnp.full((SEL_LANES - nb, TQ), NEG, F32)], axis=0)
        return bias.astype(BF16)

    tiles_per_arm = SEL_ARM_ROWS * SEL_LEN // TQ
    arms = [functools.partial(select, SEL_ARM_ROWS * (i + 1)) for i in range(SEL_LANES // SEL_ARM_ROWS)]
    bias = lax.switch(jnp.minimum(lax.div(qi, tiles_per_arm), len(arms) - 1), arms)
    qaug = jnp.concatenate([jnp.concatenate([bias] * GROUP, axis=1), qt], axis=0)

    n_full = lax.shift_right_logical(st, SEL_TRIP.bit_length() - 1)
    causal = jnp.where(lax.broadcasted_iota(jnp.int32, (TQ, 1), 0) <= q_loc, 0.0, NEG)

    def selected(static):
        def scores(keys):
            return _dot(ka_ref[0, 0, keys, :], qaug), vst_ref[0, 0, :, keys]

        def last(before):
            s_t, v_t = scores(pl.ds(pl.multiple_of(st - before, TQ), before + TQ))
            own = s_t[before:] + causal
            s_t = jnp.concatenate([s_t[:before], own], axis=0) if before else own
            if static:
                return _dot(v_t, _exp_terms(s_t, True).astype(BF16))
            m = jnp.max(s_t, axis=0, keepdims=True)
            return m, _dot(v_t, jnp.exp2(s_t - m).astype(BF16))

        rem = lax.shift_right_logical(st & (SEL_TRIP - 1), TQ.bit_length() - 1)
        first = lax.switch(rem, [functools.partial(last, i * TQ) for i in range(SEL_TRIP // TQ)])
        trip_keys = lambda c: pl.ds(pl.multiple_of(c * SEL_TRIP, SEL_TRIP), SEL_TRIP)
        if static:
            def trip(c, acc):
                s_t, v_t = scores(trip_keys(c))
                return acc + _dot(v_t, jnp.exp2(s_t).astype(BF16))

            return lax.fori_loop(0, n_full, trip, first)

        def trip(c, carry):
            m, acc = carry
            s_t, v_t = scores(trip_keys(c))
            mn = jnp.maximum(m, jnp.max(s_t, axis=0, keepdims=True))
            return mn, jnp.exp2(m - mn) * acc + _dot(v_t, jnp.exp2(s_t - mn).astype(BF16))

        return lax.fori_loop(0, n_full, trip, first)[1]

    def window(static):
        wkeys = pl.ds(pl.multiple_of(st, TQ), WINDOW + TQ)
        s_t = _dot(kw_ref[0, 0, wkeys, :], qt)
        e = jnp.concatenate([_exp_terms(s_t[:, r * TQ:(r + 1) * TQ] + wb_ref[0], static)
                             for r in range(GROUP)], axis=1)
        return _dot(vwt_ref[0, 0, :, wkeys], e.astype(BF16))

    o_slc = _normalise(both(selected))
    o_win = _normalise(both(window))

    sg = _sigmoid(gt_ref[0]).T
    heads = []
    for r in range(GROUP):
        acc = jnp.zeros((HEAD_DIM, TQ), F32)
        for j, y in enumerate((o_cmp, o_slc, o_win)):
            acc = acc + sg[3 * r + j:3 * r + j + 1, :] * y[:, r * TQ:(r + 1) * TQ]
        heads.append(acc)
    o_ref[0] = jnp.concatenate(heads, axis=0).T


def _attention(static, q_t, gates3, kcmp, vcmp_t, overlap_t, kaug, vslc_t, kwin, vwin_t):
    b, _, n_tiles, _, cols = q_t.shape
    s = n_tiles * TQ
    ncp = kcmp.shape[2]
    sp = kwin.shape[2]
    per_bg = lambda shape: pl.BlockSpec((1, 1) + shape, lambda i, j, k: (i, j, 0, 0))
    return pl.pallas_call(
        functools.partial(_attn_kernel, static=static),
        grid=(b, N_KV, n_tiles),
        in_specs=[pl.BlockSpec((1, 1, 1, QK_LANES, cols), lambda i, j, k: (i, j, k, 0, 0)),
                  pl.BlockSpec((1, TQ, GATE_LANES), lambda i, j, k: (i, k, j)),
                  per_bg((ncp, QK_LANES)), per_bg((HEAD_DIM, ncp)), _full((SEL_LANES, ncp)),
                  per_bg((s, SEL_LANES + QK_LANES)), per_bg((V_LANES, s)),
                  per_bg((sp, QK_LANES)), per_bg((V_LANES, sp)),
                  pl.BlockSpec((1, WINDOW + TQ, TQ), lambda i, j, k: (jnp.minimum(k, WINDOW // TQ), 0, 0))],
        out_specs=pl.BlockSpec((1, TQ, KV_W), lambda i, j, k: (i, k, j)),
        out_shape=jax.ShapeDtypeStruct((b, s, N_HEADS * HEAD_DIM), F32),
        compiler_params=_cparams("parallel", "parallel", "arbitrary"),
        name="nsa_attention" if static else "nsa_attention_maxshift",
    )(q_t, gates3, kcmp, vcmp_t, overlap_t, kaug, vslc_t, kwin, vwin_t, _band_biases())


def _band_biases():
    q = np.arange(TQ)[None, :]
    kw = np.arange(WINDOW + TQ)[:, None] - WINDOW
    in_band = (kw <= q) & (kw > q - WINDOW)
    bands = [np.where(in_band & (kw + tile * TQ >= 0), 0.0, NEG) for tile in range(WINDOW // TQ + 1)]
    return jnp.asarray(np.stack(bands), F32)


def _overlap_matrix_t(s):
    ncp = s // CMP_STRIDE
    i = np.arange(ncp)[None, :]
    j = np.arange(SEL_LANES)[:, None]
    lo = np.maximum(i * CMP_STRIDE, j * SEL_LEN)
    hi = np.minimum(i * CMP_STRIDE + CMP_LEN, (j + 1) * SEL_LEN)
    ov = np.maximum(hi - lo, 0) / CMP_STRIDE
    ov = ov * (i < (s - CMP_LEN) // CMP_STRIDE + 1) * (j < s // SEL_LEN)
    return jnp.asarray(ov, dtype=BF16)


def _static_ok(q_g, *k_gs):
    kmax = functools.reduce(jnp.maximum, [jnp.max(jnp.abs(g)) for g in k_gs])
    return HEAD_DIM * SCALE * jnp.max(jnp.abs(q_g)) * kmax < STATIC_SCORE_BOUND


def _merge_kernel(x_ref, yr_ref, o_ref, mg_ref, wa_ref, wo_ref, g2_ref, x1_ref, hn_ref):
    y_attn = _dot(o_ref[...].astype(BF16), wa_ref[...])
    mg = mg_ref[...]
    m = _sigmoid(mg[:, :D_MODEL]) * yr_ref[...] + _sigmoid(mg[:, D_MODEL:]) * y_attn
    x1 = x_ref[...] + _dot(m.astype(BF16), wo_ref[...])
    x1_ref[...] = x1
    hn_ref[...] = _rms(x1, g2_ref[...]).astype(BF16)


def _merge(x2, y_rnn, o_attn, mg, w_attn_out, w_out, norm2_g):
    t = x2.shape[0]
    tm = _tile(t, 512)
    rowblk = lambda n: pl.BlockSpec((tm, n), lambda i: (i, 0))
    return pl.pallas_call(
        _merge_kernel,
        grid=(t // tm,),
        in_specs=[rowblk(D_MODEL), rowblk(D_MODEL), rowblk(D_MODEL), rowblk(2 * D_MODEL),
                  _full((D_MODEL, D_MODEL)), _full((D_MODEL, D_MODEL)), _full((1, D_MODEL))],
        out_specs=[rowblk(D_MODEL), rowblk(D_MODEL)],
        out_shape=[jax.ShapeDtypeStruct((t, D_MODEL), F32), jax.ShapeDtypeStruct((t, D_MODEL), BF16)],
        compiler_params=_cparams("parallel"),
        name="merge",
    )(x2, y_rnn, o_attn, mg, w_attn_out.astype(BF16), w_out.astype(BF16), norm2_g.reshape(1, D_MODEL))


def _ffn_up_kernel(hp_ref, h_ref, wg_ref, wv_ref, cwg_ref, cwv_ref, cbg_ref, cbv_ref, act_ref, up_ref,
                   *, tm, tiles_per_seq):
    first = (pl.program_id(0) % tiles_per_seq) == 0
    keep = jnp.where(first, 0.0, 1.0)

    def conv(w_ref, cw_ref, cb_ref):
        up_ref[0:HALO, :] = _dot(hp_ref[...], w_ref[...]) * keep
        up_ref[HALO:HALO + tm, :] = _dot(h_ref[...], w_ref[...])
        cw = cw_ref[...]
        y = cb_ref[...] + cw[FFN_CONV_W - 1:FFN_CONV_W, :] * up_ref[HALO:HALO + tm, :]
        for k in range(FFN_CONV_W - 1):
            off = HALO - (FFN_CONV_W - 1) + k
            y = y + cw[k:k + 1, :] * up_ref[off:off + tm, :]
        return y

    gate = conv(wg_ref, cwg_ref, cbg_ref)
    val = conv(wv_ref, cwv_ref, cbv_ref)
    act_ref[...] = (_gelu(gate) * val).astype(BF16)


def _ffn_up(hn, w_up, conv_w, conv_b, s):
    t = hn.shape[0]
    tm = _tile(s, 1024)
    tn = 512
    nj = D_FF // tn
    wb = w_up.astype(BF16)
    cb = conv_b.reshape(1, 2 * D_FF)
    return pl.pallas_call(
        functools.partial(_ffn_up_kernel, tm=tm, tiles_per_seq=s // tm),
        grid=(t // tm, nj),
        in_specs=[pl.BlockSpec((HALO, D_MODEL), lambda i, j: (jnp.maximum(i * (tm // HALO) - 1, 0), 0)),
                  pl.BlockSpec((tm, D_MODEL), lambda i, j: (i, 0)),
                  pl.BlockSpec((D_MODEL, tn), lambda i, j: (0, j)),
                  pl.BlockSpec((D_MODEL, tn), lambda i, j: (0, j + nj)),
                  pl.BlockSpec((FFN_CONV_W, tn), lambda i, j: (0, j)),
                  pl.BlockSpec((FFN_CONV_W, tn), lambda i, j: (0, j + nj)),
                  pl.BlockSpec((1, tn), lambda i, j: (0, j)),
                  pl.BlockSpec((1, tn), lambda i, j: (0, j + nj))],
        out_specs=pl.BlockSpec((tm, tn), lambda i, j: (i, j)),
        out_shape=jax.ShapeDtypeStruct((t, D_FF), BF16),
        scratch_shapes=[pltpu.VMEM((tm + HALO, tn), F32)],
        compiler_params=_cparams("parallel", "arbitrary"),
        name="ffn_up",
    )(hn, hn, wb, wb, conv_w, conv_w, cb, cb)


def _ffn_down_kernel(x1_ref, act_ref, w_ref, o_ref):
    o_ref[...] = x1_ref[...] + _dot(act_ref[...], w_ref[...])


def _ffn_down(x1, act, w_down):
    t = x1.shape[0]
    tm = _tile(t, 512)
    return pl.pallas_call(
        _ffn_down_kernel,
        grid=(t // tm,),
        in_specs=[pl.BlockSpec((tm, D_MODEL), lambda i: (i, 0)), pl.BlockSpec((tm, D_FF), lambda i: (i, 0)),
                  _full((D_FF, D_MODEL))],
        out_specs=pl.BlockSpec((tm, D_MODEL), lambda i: (i, 0)),
        out_shape=jax.ShapeDtypeStruct((t, D_MODEL), F32),
        compiler_params=_cparams("parallel"),
        name="ffn_down",
    )(x1, act, w_down.astype(BF16))


def _cmp_rows(a, b, s):
    a = a.reshape(b, s // CMP_STRIDE, CMP_STRIDE, N_KV, HEAD_DIM).transpose(0, 3, 1, 2, 4)
    return a.reshape(b, N_KV, s // CMP_STRIDE, CMP_STRIDE * HEAD_DIM)


def kernel(x, norm1_g, w_in, rnn_conv_w, rnn_conv_b, rg_w_a, rg_b_a, rg_w_x, rg_b_x, lru_lambda, w_rnn_out, q_norm_g, k_cmp_norm_g, k_slc_norm_g, k_win_norm_g, cmp_pe_k, cmp_w1_k, cmp_w2_k, cmp_pe_v, cmp_w1_v, cmp_w2_v, w_attn_out, w_out, norm2_g, w_up, ffn_conv_w, ffn_conv_b, w_down):
    b, s, d = x.shape
    assert d == D_MODEL and s % SEL_TRIP == 0 and s // SEL_LEN <= SEL_LANES
    t = b * s
    x2 = x.reshape(t, d)

    rx, rgate, kvc, gates, mg, qpad, kaug_t, vslc, kwin_t, vwin = _in_proj(
        x2, b, s, norm1_g, w_in, q_norm_g, k_slc_norm_g, k_win_norm_g)

    y_rnn = _rglru(rx.reshape(b, s, D_RNN), rgate.reshape(b, s, D_RNN), rnn_conv_w, rnn_conv_b,
                   rg_w_a, rg_b_a, rg_w_x, rg_b_x, lru_lambda, w_rnn_out).reshape(t, d)

    kcmp, vcmp = _compress(_cmp_rows(kvc[:, :KV_W], b, s), _cmp_rows(kvc[:, KV_W:], b, s),
                           cmp_pe_k, cmp_pe_v, cmp_w1_k, cmp_w1_v, cmp_w2_k, cmp_w2_v, k_cmp_norm_g)
    kcmp_pad = jnp.pad(kcmp.astype(BF16), ((0, 0), (0, 0), (0, 0), (0, QK_LANES - HEAD_DIM)))
    vcmp_t = vcmp.transpose(0, 1, 3, 2).astype(BF16)
    attn_args = (qpad, gates.reshape(b, s, N_KV * GATE_LANES), kcmp_pad, vcmp_t, _overlap_matrix_t(s),
                 kaug_t, vslc, kwin_t, vwin)
    o_attn = lax.cond(_static_ok(q_norm_g, k_cmp_norm_g, k_slc_norm_g, k_win_norm_g),
                      functools.partial(_attention, True), functools.partial(_attention, False),
                      *attn_args).reshape(t, d)

    x1, hn = _merge(x2, y_rnn, o_attn, mg, w_attn_out, w_out, norm2_g)
    act = _ffn_up(hn, w_up, ffn_conv_w, ffn_conv_b, s)
    return _ffn_down(x1, act, w_down).reshape(b, s, d)
```

```python
import functools

import numpy as np
import jax
import jax.numpy as jnp
from jax import lax
from jax.experimental import pallas as pl
from jax.experimental.pallas import tpu as pltpu

F32 = jnp.float32
BF16 = jnp.bfloat16

D_MODEL = 1024
D_RNN = 1344
RNN_BLOCKS = 16
RNN_CONV_W = 4
LRU_C = 8.0
N_HEADS = 16
N_KV = 4
HEAD_DIM = 64
GROUP = N_HEADS // N_KV
KV_W = N_KV * HEAD_DIM
CMP_LEN = 32
CMP_STRIDE = 16
CMP_HIDDEN = 128
SEL_LEN = 64
N_SELECT = 16
N_FORCED = 3
WINDOW = 512
D_FF = 3 * D_MODEL
FFN_CONV_W = 3
EPS = 1e-6
NEG = -1e30
FORCE = 1e6
SCALE = HEAD_DIM ** -0.5

SEL_LANES = 128
V_LANES = 128
QK_LANES = 128
GATE_LANES = 128
TQ = 256
SEL_TRIP = 1024
SEL_ARM_ROWS = 32
LOG2E = float(np.log2(np.e))
HALO = 8
VMEM_LIMIT = 56 * 1024 * 1024
STATIC_SCORE_BOUND = 40.0


def _cparams(*sem):
    return pltpu.CompilerParams(dimension_semantics=sem, vmem_limit_bytes=VMEM_LIMIT)


def _gelu(x):
    c = float(np.sqrt(2.0 / np.pi))
    half = 0.5 * x
    return half + half * jnp.tanh(x * (c + (c * 0.044715) * (x * x)))


def _sigmoid(x):
    return 0.5 + 0.5 * jnp.tanh(0.5 * x)


def _rms(x, g):
    ms = jnp.mean(x * x, axis=-1, keepdims=True)
    return x * lax.rsqrt(ms + EPS) * g


def _dot(a, b):
    return jnp.dot(a, b, preferred_element_type=F32)


def _dot_nt(a, b):
    return lax.dot_general(a, b, (((1,), (1,)), ((), ())), preferred_element_type=F32)


def _split_bf16(x):
    hi = x.astype(BF16)
    lo = (x - hi.astype(F32)).astype(BF16)
    return hi, lo


def _tile(n, want):
    t = min(n, want)
    while n % t:
        t //= 2
    return t


def _full(shape):
    nd = len(shape)
    return pl.BlockSpec(shape, lambda *_: (0,) * nd)


def _in_proj_kernel(x_ref, g_ref, w_rx, w_rg, w_q, w_kv, w_g, w_mg, qg_ref, gs_ref, gw_ref, eye_ref,
                    kw0_ref, vw0_ref,
                    o_rx, o_rg, o_cmp, o_g, o_mg, q_ref, ka_ref, vs_ref, kw_ref, vw_ref, *, tiles_per_seq):
    del kw0_ref, vw0_ref
    h = _rms(x_ref[...], g_ref[...]).astype(BF16)

    q = _dot(h, w_q[...])
    kv = _dot(h, w_kv[...])
    o_rx[...] = _dot(h, w_rx[...])
    for g in range(N_KV):
        for r in range(GROUP):
            lo = (g * GROUP + r) * HEAD_DIM
            y = _rms(q[:, lo:lo + HEAD_DIM], qg_ref[...]) * (SCALE * LOG2E)
            q_ref[0, g, 0, :, r * TQ:(r + 1) * TQ] = _dot_nt(eye_ref[...], y.astype(BF16)).astype(BF16)

    o_rg[...] = _dot(h, w_rg[...])
    o_mg[...] = _dot(h, w_mg[...])
    o_cmp[...] = kv[:, :2 * KV_W]
    pos = (pl.program_id(0) % tiles_per_seq) * TQ + lax.broadcasted_iota(jnp.int32, (TQ, 1), 0)
    blk = lax.shift_right_logical(pos, SEL_LEN.bit_length() - 1)
    onehot = (lax.broadcasted_iota(jnp.int32, (1, SEL_LANES), 1) == blk).astype(BF16)
    zeros = jnp.zeros((TQ, QK_LANES - HEAD_DIM), BF16)
    ones = jnp.ones((V_LANES - HEAD_DIM, TQ), BF16)
    for g in range(N_KV):
        lo = g * HEAD_DIM
        ka_ref[0, g, :, 0:SEL_LANES] = onehot
        ka_ref[0, g, :, SEL_LANES:SEL_LANES + HEAD_DIM] = _rms(
            kv[:, 2 * KV_W + lo:2 * KV_W + lo + HEAD_DIM], gs_ref[...]).astype(BF16)
        ka_ref[0, g, :, SEL_LANES + HEAD_DIM:SEL_LANES + QK_LANES] = zeros
        kw_ref[0, g, :, 0:HEAD_DIM] = _rms(kv[:, 4 * KV_W + lo:4 * KV_W + lo + HEAD_DIM], gw_ref[...]).astype(BF16)
        kw_ref[0, g, :, HEAD_DIM:QK_LANES] = zeros
        for src, dst in ((3, vs_ref), (5, vw_ref)):
            v = kv[:, src * KV_W + lo:src * KV_W + lo + HEAD_DIM].astype(BF16)
            dst[0, g, 0:HEAD_DIM, :] = _dot_nt(eye_ref[0:HEAD_DIM, :], v).astype(BF16)
            dst[0, g, HEAD_DIM:V_LANES, :] = ones

    o_g[...] = _dot(h, w_g[...])


def _in_proj(x2, b, s, norm1_g, w_in, q_g, g_slc, g_win):
    t = x2.shape[0]
    nq = s // TQ
    pad_blocks = WINDOW // TQ
    kvw = 6 * KV_W
    o0 = 2 * D_RNN
    o1 = o0 + N_HEADS * HEAD_DIM
    o2 = o1 + kvw
    o3 = o2 + 3 * N_HEADS
    wb = w_in.astype(BF16)
    w_g = jnp.pad(wb[:, o2:o3].reshape(D_MODEL, N_KV, 3 * GROUP), ((0, 0), (0, 0), (0, GATE_LANES - 3 * GROUP)))
    w_g = w_g.reshape(D_MODEL, N_KV * GATE_LANES)
    ws = [wb[:, :D_RNN], wb[:, D_RNN:o0], wb[:, o0:o1], wb[:, o1:o2], w_g, wb[:, o3:]]
    eye = jnp.asarray(np.eye(QK_LANES, HEAD_DIM), BF16)
    gain = lambda v: v.reshape(1, HEAD_DIM)
    rows = lambda n: pl.BlockSpec((TQ, n), lambda i: (i, 0))
    flat = lambda n: jax.ShapeDtypeStruct((t, n), F32)
    kwin0 = jnp.zeros((b, N_KV, s + WINDOW, QK_LANES), BF16)
    vwin0 = jnp.zeros((b, N_KV, V_LANES, s + WINDOW), BF16)
    n_in = 14
    return pl.pallas_call(
        functools.partial(_in_proj_kernel, tiles_per_seq=nq),
        grid=(t // TQ,),
        in_specs=[rows(D_MODEL), _full((1, D_MODEL))] + [_full(w.shape) for w in ws]
        + [_full((1, HEAD_DIM))] * 3 + [_full((QK_LANES, HEAD_DIM))]
        + [pl.BlockSpec(memory_space=pl.ANY)] * 2,
        out_specs=[rows(D_RNN), rows(D_RNN), rows(2 * KV_W), rows(N_KV * GATE_LANES), rows(2 * D_MODEL),
                   pl.BlockSpec((1, N_KV, 1, QK_LANES, GROUP * TQ), lambda i: (i // nq, 0, i % nq, 0, 0)),
                   pl.BlockSpec((1, N_KV, TQ, SEL_LANES + QK_LANES), lambda i: (i // nq, 0, i % nq, 0)),
                   pl.BlockSpec((1, N_KV, V_LANES, TQ), lambda i: (i // nq, 0, 0, i % nq)),
                   pl.BlockSpec((1, N_KV, TQ, QK_LANES), lambda i: (i // nq, 0, i % nq + pad_blocks, 0)),
                   pl.BlockSpec((1, N_KV, V_LANES, TQ), lambda i: (i // nq, 0, 0, i % nq + pad_blocks))],
        out_shape=[flat(D_RNN), flat(D_RNN), flat(2 * KV_W), flat(N_KV * GATE_LANES), flat(2 * D_MODEL),
                   jax.ShapeDtypeStruct((b, N_KV, nq, QK_LANES, GROUP * TQ), BF16),
                   jax.ShapeDtypeStruct((b, N_KV, s, SEL_LANES + QK_LANES), BF16),
                   jax.ShapeDtypeStruct((b, N_KV, V_LANES, s), BF16),
                   jax.ShapeDtypeStruct(kwin0.shape, BF16), jax.ShapeDtypeStruct(vwin0.shape, BF16)],
        input_output_aliases={n_in - 2: 8, n_in - 1: 9},
        compiler_params=_cparams("parallel"),
        name="in_proj",
    )(x2, norm1_g.reshape(1, D_MODEL), *ws, gain(q_g), gain(g_slc), gain(g_win), eye, kwin0, vwin0)


def _rglru_kernel(rx_ref, rg_ref, cw_ref, cb_ref, wa_ref, ba_ref, wx_ref, bx_ref, lam_ref, wo_ref,
                  y_ref, xs_ref, carry_ref, *, tt):
    ti = pl.program_id(1)

    @pl.when(ti == 0)
    def _():
        xs_ref[0:HALO, :] = jnp.zeros((HALO, D_RNN), F32)
        carry_ref[...] = jnp.zeros((HALO, D_RNN), F32)

    xs_ref[HALO:HALO + tt, :] = rx_ref[0]
    cw = cw_ref[...]
    x = cb_ref[...] + cw[RNN_CONV_W - 1:RNN_CONV_W, :] * xs_ref[HALO:HALO + tt, :]
    for k in range(RNN_CONV_W - 1):
        off = HALO - (RNN_CONV_W - 1) + k
        x = x + cw[k:k + 1, :] * xs_ref[off:off + tt, :]
    xs_ref[0:HALO, :] = xs_ref[tt:tt + HALO, :]

    xb = x.astype(BF16)
    r = _sigmoid(_dot(xb, wa_ref[...]) + ba_ref[...])
    gi = _sigmoid(_dot(xb, wx_ref[...]) + bx_ref[...])
    lam = lam_ref[...]
    log_sig = jnp.minimum(lam, 0.0) - jnp.log1p(jnp.exp(-jnp.abs(lam)))
    log_a = LRU_C * r * log_sig
    a = jnp.exp(log_a)
    y = 1.0 - a * a
    u = jnp.where(y > 0.0, y * lax.rsqrt(y), 0.0) * (gi * x)

    a = a.reshape(tt // HALO, HALO, D_RNN)
    u = u.reshape(tt // HALO, HALO, D_RNN)
    sub = lax.broadcasted_iota(jnp.int32, (1, HALO, 1), 1)
    for d in (1, 2, 4):
        keep = sub >= d
        a_sh = pltpu.roll(a, d, 1)
        u_sh = pltpu.roll(u, d, 1)
        u = jnp.where(keep, a * u_sh + u, u)
        a = jnp.where(keep, a * a_sh, a)
    carry = carry_ref[...]
    hs = []
    for g in range(tt // HALO):
        h = a[g] * carry + u[g]
        hs.append(h)
        carry = jnp.broadcast_to(h[HALO - 1:HALO, :], (HALO, D_RNN))
    carry_ref[...] = carry

    hg = (jnp.concatenate(hs, axis=0) * _gelu(rg_ref[0])).astype(BF16)
    y_ref[0] = _dot(hg, wo_ref[...])


def _block_diag(w):
    n, c, d = w.shape
    return jnp.einsum("ncd,nm->ncmd", w, jnp.eye(n, dtype=w.dtype)).reshape(n * c, n * d)


def _rglru(rx3, rg3, conv_w, conv_b, w_a, b_a, w_x, b_x, lam, w_out):
    b, s, _ = rx3.shape
    tt = _tile(s, 256)
    row = lambda v: v.reshape(1, D_RNN)
    return pl.pallas_call(
        functools.partial(_rglru_kernel, tt=tt),
        grid=(b, s // tt),
        in_specs=[pl.BlockSpec((1, tt, D_RNN), lambda i, j: (i, j, 0)),
                  pl.BlockSpec((1, tt, D_RNN), lambda i, j: (i, j, 0)),
                  _full((RNN_CONV_W, D_RNN)), _full((1, D_RNN)),
                  _full((D_RNN, D_RNN)), _full((1, D_RNN)),
                  _full((D_RNN, D_RNN)), _full((1, D_RNN)),
                  _full((1, D_RNN)), _full((D_RNN, D_MODEL))],
        out_specs=pl.BlockSpec((1, tt, D_MODEL), lambda i, j: (i, j, 0)),
        out_shape=jax.ShapeDtypeStruct((b, s, D_MODEL), F32),
        scratch_shapes=[pltpu.VMEM((tt + HALO, D_RNN), F32), pltpu.VMEM((HALO, D_RNN), F32)],
        compiler_params=_cparams("arbitrary", "arbitrary"),
        name="rglru",
    )(rx3, rg3, conv_w, row(conv_b), _block_diag(w_a).astype(BF16), row(b_a),
      _block_diag(w_x).astype(BF16), row(b_x), row(lam), w_out.astype(BF16))


def _compress_kernel(rk_ref, rv_ref, pek_ref, pev_ref, w1k_ref, w1v_ref, w2k_ref, w2v_ref, gk_ref,
                     kc_ref, vc_ref):
    half = CMP_STRIDE * HEAD_DIM
    ncp = rk_ref.shape[2]

    def mlp(r_ref, pe_ref, w1_ref, w2_ref):
        r = r_ref[0, 0]
        pe = pe_ref[...]
        lo = _dot((r + pe[:, :half]).astype(BF16), w1_ref[:half, :])
        hi = _dot((r + pe[:, half:]).astype(BF16), w1_ref[half:, :])
        hid = lo + pltpu.roll(hi, ncp - 1, 0)
        return _dot(_gelu(hid).astype(BF16), w2_ref[...])

    kc_ref[0, 0] = _rms(mlp(rk_ref, pek_ref, w1k_ref, w2k_ref), gk_ref[...])
    vc_ref[0, 0] = mlp(rv_ref, pev_ref, w1v_ref, w2v_ref)


def _compress(rk, rv, pe_k, pe_v, w1k, w1v, w2k, w2v, g_cmp):
    b, g, ncp, wide = rk.shape
    blk = pl.BlockSpec((1, 1, ncp, wide), lambda i, j: (i, j, 0, 0))
    out = pl.BlockSpec((1, 1, ncp, HEAD_DIM), lambda i, j: (i, j, 0, 0))
    w1 = _full((CMP_LEN * HEAD_DIM, CMP_HIDDEN))
    w2 = _full((CMP_HIDDEN, HEAD_DIM))
    pe = _full((1, CMP_LEN * HEAD_DIM))
    return pl.pallas_call(
        _compress_kernel,
        grid=(b, g),
        in_specs=[blk, blk, pe, pe, w1, w1, w2, w2, _full((1, HEAD_DIM))],
        out_specs=[out, out],
        out_shape=[jax.ShapeDtypeStruct((b, g, ncp, HEAD_DIM), F32)] * 2,
        compiler_params=_cparams("parallel", "parallel"),
        name="compress",
    )(rk, rv, pe_k.reshape(1, -1), pe_v.reshape(1, -1), w1k.astype(BF16), w1v.astype(BF16),
      w2k.astype(BF16), w2v.astype(BF16), g_cmp.reshape(1, HEAD_DIM))


def _exp_terms(sm, static):
    if static:
        return jnp.exp2(sm)
    return jnp.exp2(sm - jnp.max(sm, axis=0, keepdims=True))


def _normalise(acc):
    return acc[:HEAD_DIM] * (1.0 / acc[HEAD_DIM:])


def _attn_kernel(q_ref, gt_ref, kc_ref, vct_ref, ovt_ref, ka_ref, vst_ref, kw_ref, vwt_ref, wb_ref, o_ref,
                 *, static):
    qi = pl.program_id(2)
    st = qi * TQ
    cols = GROUP * TQ
    ncp = kc_ref.shape[2]
    both = lambda f: f(static)

    qt = q_ref[0, 0, 0]
    q_loc = lax.broadcasted_iota(jnp.int32, (1, cols), 1) & (TQ - 1)
    t_col = st + q_loc

    def compressed(static):
        def over(n):
            cidx = lax.broadcasted_iota(jnp.int32, (n, 1), 0)
            valid = (cidx * CMP_STRIDE + (CMP_LEN - 1)) <= t_col
            sm = jnp.where(valid, _dot(kc_ref[0, 0, 0:n, :], qt), NEG)
            e = _exp_terms(sm, True) if static else jnp.where(valid, _exp_terms(sm, False), 0.0)
            l = jnp.sum(e, axis=0, keepdims=True)
            p = e * (1.0 / jnp.where(l > 0.0, l, 1.0))
            psum = p[:, 0:TQ] + p[:, TQ:2 * TQ] + p[:, 2 * TQ:3 * TQ] + p[:, 3 * TQ:4 * TQ]
            if n < ncp:
                psum = jnp.concatenate([psum, jnp.zeros((ncp - n, TQ), F32)], axis=0)
            return _dot(vct_ref[0, 0, :, 0:n], p.astype(BF16)), psum

        half = ncp // 2
        if half % 128:
            return over(ncp)
        return lax.cond(st + TQ <= half * CMP_STRIDE + CMP_LEN - 1, lambda: over(half), lambda: over(ncp))

    o_cmp, psum = both(compressed)

    p_hi, p_lo = _split_bf16(psum)
    imp_t = _dot(ovt_ref[...], p_hi) + _dot(ovt_ref[...], p_lo)
    cur = lax.shift_right_logical(st + lax.broadcasted_iota(jnp.int32, (1, TQ), 1), SEL_LEN.bit_length() - 1)

    def select(nb):
        jb = lax.broadcasted_iota(jnp.int32, (nb, 1), 0)
        jbf = jb.astype(F32)
        forced = (jb == 0) | (jb == cur) | (jb == cur - 1)
        sel = forced
        work = jnp.where((jb <= cur) & jnp.logical_not(forced), imp_t[:nb], -FORCE)
        for _ in range(N_SELECT - N_FORCED):
            mx = jnp.max(work, axis=0, keepdims=True)
            first = jnp.min(jnp.where(work == mx, jbf, float(SEL_LANES)), axis=0, keepdims=True)
            pick = jbf == first
            sel = sel | pick
            work = jnp.where(pick, -jnp.inf, work)
        bias = jnp.where(sel & (jb <= cur), 0.0, NEG)
        if nb < SEL_LANES:
            bias = jnp.concatenate([bias, jnp.full((SEL_LANES - nb, TQ), NEG, F32)], axis=0)
        return bias.astype(BF16)

    tiles_per_arm = SEL_ARM_ROWS * SEL_LEN // TQ
    arms = [functools.partial(select, SEL_ARM_ROWS * (i + 1)) for i in range(SEL_LANES // SEL_ARM_ROWS)]
    bias = lax.switch(jnp.minimum(lax.div(qi, tiles_per_arm), len(arms) - 1), arms)
    qaug = jnp.concatenate([jnp.concatenate([bias] * GROUP, axis=1), qt], axis=0)

    n_full = lax.shift_right_logical(st, SEL_TRIP.bit_length() - 1)
    causal = jnp.where(lax.broadcasted_iota(jnp.int32, (TQ, 1), 0) <= q_loc, 0.0, NEG)

    def selected(static):
        def scores(keys):
            return _dot(ka_ref[0, 0, keys, :], qaug), vst_ref[0, 0, :, keys]

        def last(before):
            s_t, v_t = scores(pl.ds(pl.multiple_of(st - before, TQ), before + TQ))
            own = s_t[before:] + causal
            s_t = jnp.concatenate([s_t[:before], own], axis=0) if before else own
            if static:
                return _dot(v_t, _exp_terms(s_t, True).astype(BF16))
            m = jnp.max(s_t, axis=0, keepdims=True)
            return m, _dot(v_t, jnp.exp2(s_t - m).astype(BF16))

        rem = lax.shift_right_logical(st & (SEL_TRIP - 1), TQ.bit_length() - 1)
        first = lax.switch(rem, [functools.partial(last, i * TQ) for i in range(SEL_TRIP // TQ)])
        trip_keys = lambda c: pl.ds(pl.multiple_of(c * SEL_TRIP, SEL_TRIP), SEL_TRIP)
        if static:
            def trip(c, acc):
                s_t, v_t = scores(trip_keys(c))
                return acc + _dot(v_t, jnp.exp2(s_t).astype(BF16))

            return lax.fori_loop(0, n_full, trip, first)

        def trip(c, carry):
            m, acc = carry
            s_t, v_t = scores(trip_keys(c))
            mn = jnp.maximum(m, jnp.max(s_t, axis=0, keepdims=True))
            return mn, jnp.exp2(m - mn) * acc + _dot(v_t, jnp.exp2(s_t - mn).astype(BF16))

        return lax.fori_loop(0, n_full, trip, first)[1]

    def window(static):
        wkeys = pl.ds(pl.multiple_of(st, TQ), WINDOW + TQ)
        s_t = _dot(kw_ref[0, 0, wkeys, :], qt)
        e = jnp.concatenate([_exp_terms(s_t[:, r * TQ:(r + 1) * TQ] + wb_ref[0], static)
                             for r in range(GROUP)], axis=1)
        return _dot(vwt_ref[0, 0, :, wkeys], e.astype(BF16))

    o_slc = _normalise(both(selected))
    o_win = _normalise(both(window))

    sg = _sigmoid(gt_ref[0]).T
    heads = []
    for r in range(GROUP):
        acc = jnp.zeros((HEAD_DIM, TQ), F32)
        for j, y in enumerate((o_cmp, o_slc, o_win)):
            acc = acc + sg[3 * r + j:3 * r + j + 1, :] * y[:, r * TQ:(r + 1) * TQ]
        heads.append(acc)
    o_ref[0] = jnp.concatenate(heads, axis=0).T.astype(BF16)


def _attention(static, q_t, gates3, kcmp, vcmp_t, overlap_t, kaug, vslc_t, kwin, vwin_t):
    b, _, n_tiles, _, cols = q_t.shape
    s = n_tiles * TQ
    ncp = kcmp.shape[2]
    sp = kwin.shape[2]
    per_bg = lambda shape: pl.BlockSpec((1, 1) + shape, lambda i, j, k: (i, j, 0, 0))
    return pl.pallas_call(
        functools.partial(_attn_kernel, static=static),
        grid=(b, N_KV, n_tiles),
        in_specs=[pl.BlockSpec((1, 1, 1, QK_LANES, cols), lambda i, j, k: (i, j, k, 0, 0)),
                  pl.BlockSpec((1, TQ, GATE_LANES), lambda i, j, k: (i, k, j)),
                  per_bg((ncp, QK_LANES)), per_bg((HEAD_DIM, ncp)), _full((SEL_LANES, ncp)),
                  per_bg((s, SEL_LANES + QK_LANES)), per_bg((V_LANES, s)),
                  per_bg((sp, QK_LANES)), per_bg((V_LANES, sp)),
                  pl.BlockSpec((1, WINDOW + TQ, TQ), lambda i, j, k: (jnp.minimum(k, WINDOW // TQ), 0, 0))],
        out_specs=pl.BlockSpec((1, TQ, KV_W), lambda i, j, k: (i, k, j)),
        out_shape=jax.ShapeDtypeStruct((b, s, N_HEADS * HEAD_DIM), BF16),
        compiler_params=_cparams("parallel", "parallel", "arbitrary"),
        name="nsa_attention" if static else "nsa_attention_maxshift",
    )(q_t, gates3, kcmp, vcmp_t, overlap_t, kaug, vslc_t, kwin, vwin_t, _band_biases())


def _band_biases():
    q = np.arange(TQ)[None, :]
    kw = np.arange(WINDOW + TQ)[:, None] - WINDOW
    in_band = (kw <= q) & (kw > q - WINDOW)
    bands = [np.where(in_band & (kw + tile * TQ >= 0), 0.0, NEG) for tile in range(WINDOW // TQ + 1)]
    return jnp.asarray(np.stack(bands), F32)


def _overlap_matrix_t(s):
    ncp = s // CMP_STRIDE
    i = np.arange(ncp)[None, :]
    j = np.arange(SEL_LANES)[:, None]
    lo = np.maximum(i * CMP_STRIDE, j * SEL_LEN)
    hi = np.minimum(i * CMP_STRIDE + CMP_LEN, (j + 1) * SEL_LEN)
    ov = np.maximum(hi - lo, 0) / CMP_STRIDE
    ov = ov * (i < (s - CMP_LEN) // CMP_STRIDE + 1) * (j < s // SEL_LEN)
    return jnp.asarray(ov, dtype=BF16)


def _static_ok(q_g, *k_gs):
    kmax = functools.reduce(jnp.maximum, [jnp.max(jnp.abs(g)) for g in k_gs])
    return HEAD_DIM * SCALE * jnp.max(jnp.abs(q_g)) * kmax < STATIC_SCORE_BOUND


def _merge_kernel(x_ref, yr_ref, o_ref, mg_ref, wa_ref, wo_ref, g2_ref, x1_ref, hn_ref):
    y_attn = _dot(o_ref[...], wa_ref[...])
    mg = mg_ref[...]
    m = _sigmoid(mg[:, :D_MODEL]) * yr_ref[...] + _sigmoid(mg[:, D_MODEL:]) * y_attn
    x1 = x_ref[...] + _dot(m.astype(BF16), wo_ref[...])
    x1_ref[...] = x1
    hn_ref[...] = _rms(x1, g2_ref[...]).astype(BF16)


def _merge(x2, y_rnn, o_attn, mg, w_attn_out, w_out, norm2_g):
    t = x2.shape[0]
    tm = _tile(t, 512)
    rowblk = lambda n: pl.BlockSpec((tm, n), lambda i: (i, 0))
    return pl.pallas_call(
        _merge_kernel,
        grid=(t // tm,),
        in_specs=[rowblk(D_MODEL), rowblk(D_MODEL), rowblk(D_MODEL), rowblk(2 * D_MODEL),
                  _full((D_MODEL, D_MODEL)), _full((D_MODEL, D_MODEL)), _full((1, D_MODEL))],
        out_specs=[rowblk(D_MODEL), rowblk(D_MODEL)],
        out_shape=[jax.ShapeDtypeStruct((t, D_MODEL), F32), jax.ShapeDtypeStruct((t, D_MODEL), BF16)],
        compiler_params=_cparams("parallel"),
        name="merge",
    )(x2, y_rnn, o_attn, mg, w_attn_out.astype(BF16), w_out.astype(BF16), norm2_g.reshape(1, D_MODEL))


def _ffn_up_kernel(hp_ref, h_ref, wg_ref, wv_ref, cwg_ref, cwv_ref, cbg_ref, cbv_ref, act_ref, up_ref,
                   *, tm, tiles_per_seq):
    first = (pl.program_id(0) % tiles_per_seq) == 0
    keep = jnp.where(first, 0.0, 1.0)

    def conv(w_ref, cw_ref, cb_ref):
        up_ref[0:HALO, :] = _dot(hp_ref[...], w_ref[...]) * keep
        up_ref[HALO:HALO + tm, :] = _dot(h_ref[...], w_ref[...])
        cw = cw_ref[...]
        y = cb_ref[...] + cw[FFN_CONV_W - 1:FFN_CONV_W, :] * up_ref[HALO:HALO + tm, :]
        for k in range(FFN_CONV_W - 1):
            off = HALO - (FFN_CONV_W - 1) + k
            y = y + cw[k:k + 1, :] * up_ref[off:off + tm, :]
        return y

    gate = conv(wg_ref, cwg_ref, cbg_ref)
    val = conv(wv_ref, cwv_ref, cbv_ref)
    act_ref[...] = (_gelu(gate) * val).astype(BF16)


def _ffn_up(hn, w_up, conv_w, conv_b, s):
    t = hn.shape[0]
    tm = _tile(s, 1024)
    tn = 512
    nj = D_FF // tn
    wb = w_up.astype(BF16)
    cb = conv_b.reshape(1, 2 * D_FF)
    return pl.pallas_call(
        functools.partial(_ffn_up_kernel, tm=tm, tiles_per_seq=s // tm),
        grid=(t // tm, nj),
        in_specs=[pl.BlockSpec((HALO, D_MODEL), lambda i, j: (jnp.maximum(i * (tm // HALO) - 1, 0), 0)),
                  pl.BlockSpec((tm, D_MODEL), lambda i, j: (i, 0)),
                  pl.BlockSpec((D_MODEL, tn), lambda i, j: (0, j)),
                  pl.BlockSpec((D_MODEL, tn), lambda i, j: (0, j + nj)),
                  pl.BlockSpec((FFN_CONV_W, tn), lambda i, j: (0, j)),
                  pl.BlockSpec((FFN_CONV_W, tn), lambda i, j: (0, j + nj)),
                  pl.BlockSpec((1, tn), lambda i, j: (0, j)),
                  pl.BlockSpec((1, tn), lambda i, j: (0, j + nj))],
        out_specs=pl.BlockSpec((tm, tn), lambda i, j: (i, j)),
        out_shape=jax.ShapeDtypeStruct((t, D_FF), BF16),
        scratch_shapes=[pltpu.VMEM((tm + HALO, tn), F32)],
        compiler_params=_cparams("parallel", "arbitrary"),
        name="ffn_up",
    )(hn, hn, wb, wb, conv_w, conv_w, cb, cb)


def _ffn_down_kernel(x1_ref, act_ref, w_ref, o_ref):
    o_ref[...] = x1_ref[...] + _dot(act_ref[...], w_ref[...])


def _ffn_down(x1, act, w_down):
    t = x1.shape[0]
    tm = _tile(t, 512)
    return pl.pallas_call(
        _ffn_down_kernel,
        grid=(t // tm,),
        in_specs=[pl.BlockSpec((tm, D_MODEL), lambda i: (i, 0)), pl.BlockSpec((tm, D_FF), lambda i: (i, 0)),
                  _full((D_FF, D_MODEL))],
        out_specs=pl.BlockSpec((tm, D_MODEL), lambda i: (i, 0)),
        out_shape=jax.ShapeDtypeStruct((t, D_MODEL), F32),
        compiler_params=_cparams("parallel"),
        name="ffn_down",
    )(x1, act, w_down.astype(BF16))


def _cmp_rows(a, b, s):
    a = a.reshape(b, s // CMP_STRIDE, CMP_STRIDE, N_KV, HEAD_DIM).transpose(0, 3, 1, 2, 4)
    return a.reshape(b, N_KV, s // CMP_STRIDE, CMP_STRIDE * HEAD_DIM)


def kernel(x, norm1_g, w_in, rnn_conv_w, rnn_conv_b, rg_w_a, rg_b_a, rg_w_x, rg_b_x, lru_lambda, w_rnn_out, q_norm_g, k_cmp_norm_g, k_slc_norm_g, k_win_norm_g, cmp_pe_k, cmp_w1_k, cmp_w2_k, cmp_pe_v, cmp_w1_v, cmp_w2_v, w_attn_out, w_out, norm2_g, w_up, ffn_conv_w, ffn_conv_b, w_down):
    b, s, d = x.shape
    assert d == D_MODEL and s % SEL_TRIP == 0 and s // SEL_LEN <= SEL_LANES
    t = b * s
    x2 = x.reshape(t, d)

    rx, rgate, kvc, gates, mg, qpad, kaug_t, vslc, kwin_t, vwin = _in_proj(
        x2, b, s, norm1_g, w_in, q_norm_g, k_slc_norm_g, k_win_norm_g)

    y_rnn = _rglru(rx.reshape(b, s, D_RNN), rgate.reshape(b, s, D_RNN), rnn_conv_w, rnn_conv_b,
                   rg_w_a, rg_b_a, rg_w_x, rg_b_x, lru_lambda, w_rnn_out).reshape(t, d)

    kcmp, vcmp = _compress(_cmp_rows(kvc[:, :KV_W], b, s), _cmp_rows(kvc[:, KV_W:], b, s),
                           cmp_pe_k, cmp_pe_v, cmp_w1_k, cmp_w1_v, cmp_w2_k, cmp_w2_v, k_cmp_norm_g)
    kcmp_pad = jnp.pad(kcmp.astype(BF16), ((0, 0), (0, 0), (0, 0), (0, QK_LANES - HEAD_DIM)))
    vcmp_t = vcmp.transpose(0, 1, 3, 2).astype(BF16)
    attn_args = (qpad, gates.reshape(b, s, N_KV * GATE_LANES), kcmp_pad, vcmp_t, _overlap_matrix_t(s),
                 kaug_t, vslc, kwin_t, vwin)
    o_attn = lax.cond(_static_ok(q_norm_g, k_cmp_norm_g, k_slc_norm_g, k_win_norm_g),
                      functools.partial(_attention, True), functools.partial(_attention, False),
                      *attn_args).reshape(t, d)

    x1, hn = _merge(x2, y_rnn, o_attn, mg, w_attn_out, w_out, norm2_g)
    act = _ffn_up(hn, w_up, ffn_conv_w, ffn_conv_b, s)
    return _ffn_down(x1, act, w_down).reshape(b, s, d)
```
